```python
import math
import jax, jax.numpy as jnp
from jax import lax
import numpy as np

D_MODEL = 1024
BATCH = 2
SEQ = 16384
DEPTH = 2

CHUNK = 64
QBLK = 128
FOX_HEADS = 8
FOX_HEAD_DIM = 64
FOX_WIDTH = FOX_HEADS * FOX_HEAD_DIM
DIFF_HEADS = 4
DIFF_HEAD_DIM = 64
DIFF_V_DIM = 2 * DIFF_HEAD_DIM
DIFF_WIDTH = DIFF_HEADS * DIFF_V_DIM
DIFF_QK_WIDTH = DIFF_HEADS * 2 * DIFF_HEAD_DIM
MIX_WIDTH = FOX_WIDTH + DIFF_WIDTH
IN_COLS = 3 * FOX_WIDTH + FOX_HEADS + 2 * DIFF_QK_WIDTH + DIFF_WIDTH
D_FF = 4 * D_MODEL
N_BUCKETS = 32
MAX_DISTANCE = 128
NORM_EPS = 1e-5
SUBLN_EPS = 1e-5
FORGET_BIAS_INIT = 4.0

kernel_name = "hymba_fox_diffattn_sqrelu_trunk"


def rmsnorm(x, g, eps=NORM_EPS):
    xf = x.astype(jnp.float32)
    y = xf * lax.rsqrt(jnp.mean(xf * xf, axis=-1, keepdims=True) + eps)
    return (y * g.astype(jnp.float32)).astype(x.dtype)


def t5_bucket(rel):
    half = N_BUCKETS // 2
    max_exact = half // 2
    ret = jnp.where(rel > 0, half, 0)
    n = jnp.abs(rel)
    nf = jnp.maximum(n, 1).astype(jnp.float32)
    large = max_exact + (jnp.log(nf / max_exact) / math.log(MAX_DISTANCE / max_exact)
                         * (half - max_exact)).astype(jnp.int32)
    large = jnp.minimum(large, half - 1)
    return ret + jnp.where(n < max_exact, n, large)


def to_blocks(t):
    b, h, s = t.shape[:3]
    t = t.reshape((b, h, s // QBLK, QBLK) + t.shape[3:])
    return jnp.moveaxis(t, 2, 0)


def from_blocks(t):
    nb, b, h, qb, d = t.shape
    return jnp.moveaxis(t, 0, 2).reshape(b, h, nb * qb, d)


def fox_attention(q, k, v, logf):
    s_len = q.shape[2]
    scale = 1.0 / math.sqrt(q.shape[-1])
    c = jnp.cumsum(logf, axis=-1)
    kpos = jnp.arange(s_len)
    nb = s_len // QBLK

    def body(args):
        i, qb, cb = args
        qpos = i * QBLK + jnp.arange(QBLK)
        s = jnp.einsum('bhqd,bhkd->bhqk', qb, k).astype(jnp.float32) * scale
        s = s + cb[..., :, None] - c[..., None, :]
        mask = kpos[None, :] <= qpos[:, None]
        s = jnp.where(mask, s, -jnp.inf)
        p = jax.nn.softmax(s, axis=-1)
        return jnp.einsum('bhqk,bhkd->bhqd', p.astype(v.dtype), v)

    out = lax.map(body, (jnp.arange(nb), to_blocks(q), to_blocks(c)))
    return from_blocks(out)


def diff_attention(q1, q2, k1, k2, v, lam, table):
    s_len = q1.shape[2]
    scale = 1.0 / math.sqrt(q1.shape[-1])
    kpos = jnp.arange(s_len)
    nb = s_len // QBLK
    tab = table.astype(jnp.float32)

    def body(args):
        i, q1b, q2b = args
        qpos = i * QBLK + jnp.arange(QBLK)
        rel = kpos[None, :] - qpos[:, None]
        bias = jnp.moveaxis(tab[t5_bucket(rel)], -1, 0)
        mask = (kpos // CHUNK)[None, :] <= (qpos // CHUNK)[:, None]

        def probs(qb, kk):
            s = jnp.einsum('bhqd,bhkd->bhqk', qb, kk).astype(jnp.float32) * scale + bias
            return jax.nn.softmax(jnp.where(mask, s, -jnp.inf), axis=-1)

        p = probs(q1b, k1) - lam * probs(q2b, k2)
        return jnp.einsum('bhqk,bhkd->bhqd', p.astype(v.dtype), v)

    out = lax.map(body, (jnp.arange(nb), to_blocks(q1), to_blocks(q2)))
    return from_blocks(out)


def setup_inputs(seed: int = 0) -> dict:
    key = jax.random.key(seed)
    ks = jax.random.split(key, 16)
    nrm = jax.random.normal
    f32 = jnp.float32
    return {
        "x": nrm(ks[0], (BATCH, SEQ, D_MODEL), f32),
        "w_in": nrm(ks[1], (DEPTH, D_MODEL, IN_COLS), f32) * D_MODEL ** -0.5,
        "b_f": FORGET_BIAS_INIT + 0.1 * nrm(ks[2], (DEPTH, FOX_HEADS), f32),
        "lambda_q1": 0.1 * nrm(ks[3], (DEPTH, DIFF_HEAD_DIM), f32),
        "lambda_k1": 0.1 * nrm(ks[4], (DEPTH, DIFF_HEAD_DIM), f32),
        "lambda_q2": 0.1 * nrm(ks[5], (DEPTH, DIFF_HEAD_DIM), f32),
        "lambda_k2": 0.1 * nrm(ks[6], (DEPTH, DIFF_HEAD_DIM), f32),
        "subln_g": 1.0 + 0.02 * nrm(ks[7], (DEPTH, DIFF_V_DIM), f32),
        "w_out": nrm(ks[8], (DEPTH, MIX_WIDTH, D_MODEL), f32) * MIX_WIDTH ** -0.5,
        "norm_attn_g": 1.0 + 0.02 * nrm(ks[9], (DEPTH, D_MODEL), f32),
        "norm_mlp_g": 1.0 + 0.02 * nrm(ks[10], (DEPTH, D_MODEL), f32),
        "w_mlp_in": nrm(ks[11], (DEPTH, D_MODEL, D_FF), f32) * D_MODEL ** -0.5,
        "w_mlp_out": nrm(ks[12], (DEPTH, D_FF, D_MODEL), f32) * D_FF ** -0.5,
        "rel_bias_table": 0.5 * nrm(ks[13], (N_BUCKETS, DIFF_HEADS), f32),
        "final_norm_g": 1.0 + 0.02 * nrm(ks[14], (D_MODEL,), f32),
    }


def reference(x, w_in, b_f, lambda_q1, lambda_k1, lambda_q2, lambda_k2, subln_g,
              w_out, norm_attn_g, norm_mlp_g, w_mlp_in, w_mlp_out,
              rel_bias_table, final_norm_g):
    B, S, _ = x.shape
    splits = np.cumsum([FOX_WIDTH, FOX_WIDTH, FOX_WIDTH, FOX_HEADS,
                        DIFF_QK_WIDTH, DIFF_QK_WIDTH])

    def heads(t, h):
        return t.reshape(B, S, h, -1).transpose(0, 2, 1, 3)

    for l in range(DEPTH):
        h = rmsnorm(x, norm_attn_g[l])
        z = h @ w_in[l]
        fq, fk, fv, ff, dq, dk, dv = jnp.split(z, splits, axis=-1)

        logf = jax.nn.log_sigmoid(ff.astype(jnp.float32) + b_f[l].astype(jnp.float32))
        logf = logf.transpose(0, 2, 1)
        fox_out = fox_attention(heads(fq, FOX_HEADS), heads(fk, FOX_HEADS),
                                heads(fv, FOX_HEADS), logf)
        fox_out = fox_out.transpose(0, 2, 1, 3).reshape(B, S, FOX_WIDTH)

        dq = dq.reshape(B, S, DIFF_HEADS, 2, DIFF_HEAD_DIM)
        dk = dk.reshape(B, S, DIFF_HEADS, 2, DIFF_HEAD_DIM)
        q1 = dq[..., 0, :].transpose(0, 2, 1, 3)
        q2 = dq[..., 1, :].transpose(0, 2, 1, 3)
        k1 = dk[..., 0, :].transpose(0, 2, 1, 3)
        k2 = dk[..., 1, :].transpose(0, 2, 1, 3)
        vv = heads(dv, DIFF_HEADS)
        lambda_init = 0.8 - 0.6 * math.exp(-0.3 * l)
        lam = (jnp.exp(jnp.sum(lambda_q1[l].astype(jnp.float32) * lambda_k1[l].astype(jnp.float32)))
               - jnp.exp(jnp.sum(lambda_q2[l].astype(jnp.float32) * lambda_k2[l].astype(jnp.float32)))
               + lambda_init)
        d_out = diff_attention(q1, q2, k1, k2, vv, lam, rel_bias_table)
        d_out = rmsnorm(d_out, subln_g[l], SUBLN_EPS) * (1.0 - lambda_init)
        d_out = d_out.transpose(0, 2, 1, 3).reshape(B, S, DIFF_WIDTH)

        mixed = jnp.concatenate([fox_out, d_out], axis=-1)
        x = x + mixed @ w_out[l]

        h = rmsnorm(x, norm_mlp_g[l])
        x = x + jnp.square(jax.nn.relu(h @ w_mlp_in[l])) @ w_mlp_out[l]

    return rmsnorm(x, final_norm_g)
```

```python
import functools
import math

import jax
import jax.numpy as jnp
import numpy as np
from jax import lax
from jax.experimental import pallas as pl
from jax.experimental.pallas import tpu as pltpu

D_MODEL = 1024
CHUNK = 64
FOX_HEADS = 8
HEAD_DIM = 64
FOX_WIDTH = FOX_HEADS * HEAD_DIM
DIFF_HEADS = 4
DIFF_V_DIM = 2 * HEAD_DIM
DIFF_WIDTH = DIFF_HEADS * DIFF_V_DIM
D_FF = 4 * D_MODEL
N_BUCKETS = 32
MAX_DISTANCE = 128
NORM_EPS = 1e-5
SUBLN_EPS = 1e-5

LANES = 128
LOG2E = math.log2(math.e)
NEG_BIG = -1e30
N_DECAY_PARTS = 3
ATTN_TILE = 512
ROW_TILE = 512
FF_CHUNK = 1024
VMEM_LIMIT = 48 * 1024 * 1024

_NT = (((1,), (1,)), ((), ()))
_TN = (((0,), (0,)), ((), ()))


def _rms(x, g, eps):
    return x * lax.rsqrt(jnp.mean(x * x, axis=-1, keepdims=True) + eps) * g


def _in_proj_kernel(x_ref, g_ref, wqv_ref, wk_ref, wf_ref, bf_ref, zt_ref, zk_ref, lf_ref):
    h = _rms(x_ref[...], g_ref[...], NORM_EPS).astype(jnp.bfloat16)
    zt = lax.dot_general(wqv_ref[...], h, _NT, preferred_element_type=jnp.float32)
    zt_ref[...] = zt.astype(jnp.bfloat16)
    zk_ref[...] = jnp.dot(h, wk_ref[...], preferred_element_type=jnp.float32).astype(jnp.bfloat16)
    f = jnp.dot(h, wf_ref[...], preferred_element_type=jnp.float32) + bf_ref[...]
    lf_ref[...] = jnp.minimum(f, 0.0) - jnp.log1p(jnp.exp(-jnp.abs(f)))


def _in_proj(x2d, g, wqv_t, wk, wf, bf, tm):
    n = x2d.shape[0]
    nqv = wqv_t.shape[0]
    nk = wk.shape[1]
    const = lambda i: (0, 0)
    return pl.pallas_call(
        _in_proj_kernel,
        grid=(n // tm,),
        in_specs=[
            pl.BlockSpec((tm, D_MODEL), lambda i: (i, 0)),
            pl.BlockSpec((1, D_MODEL), const),
            pl.BlockSpec((nqv, D_MODEL), const),
            pl.BlockSpec((D_MODEL, nk), const),
            pl.BlockSpec((D_MODEL, LANES), const),
            pl.BlockSpec((1, LANES), const),
        ],
        out_specs=[
            pl.BlockSpec((nqv, tm), lambda i: (0, i)),
            pl.BlockSpec((tm, nk), lambda i: (i, 0)),
            pl.BlockSpec((tm, LANES), lambda i: (i, 0)),
        ],
        out_shape=[
            jax.ShapeDtypeStruct((nqv, n), jnp.bfloat16),
            jax.ShapeDtypeStruct((n, nk), jnp.bfloat16),
            jax.ShapeDtypeStruct((n, LANES), jnp.float32),
        ],
        compiler_params=pltpu.CompilerParams(
            dimension_semantics=("arbitrary",), vmem_limit_bytes=VMEM_LIMIT),
        name="in_proj",
    )(x2d, g, wqv_t, wk, wf, bf)


def _decay_scan_kernel(x_ref, o_ref, *, rows_per_seq):
    x = x_ref[...]
    lane = lax.broadcasted_iota(jnp.int32, x.shape, 1)
    d = 1
    while d < LANES:
        x = x + jnp.where(lane >= d, pltpu.roll(x, d, 1), 0.0)
        d *= 2
    tot = jnp.broadcast_to(x[:, LANES - 1:LANES], x.shape)
    row = lax.broadcasted_iota(jnp.int32, x.shape, 0) % rows_per_seq
    inc = tot
    d = 1
    while d < rows_per_seq:
        inc = inc + jnp.where(row >= d, pltpu.roll(inc, d, 0), 0.0)
        d *= 2
    o_ref[...] = x + (inc - tot)


def _decay_scan(lf_rows, rows_per_seq):
    return pl.pallas_call(
        functools.partial(_decay_scan_kernel, rows_per_seq=rows_per_seq),
        out_shape=jax.ShapeDtypeStruct(lf_rows.shape, jnp.float32),
        compiler_params=pltpu.CompilerParams(vmem_limit_bytes=VMEM_LIMIT),
        name="decay_scan",
    )(lf_rows)


def _online_update(s, v_t, m_ref, l_ref, acc_ref, idx):
    m_prev = m_ref[idx]
    m_new = jnp.maximum(m_prev, jnp.max(s, axis=0, keepdims=True))
    p = jnp.exp2(s - m_new)
    alpha = jnp.exp2(m_prev - m_new)
    l_ref[idx] = alpha * l_ref[idx] + jnp.sum(p, axis=0, keepdims=True)
    pv = jnp.dot(v_t, p.astype(jnp.bfloat16), preferred_element_type=jnp.float32)
    acc_ref[idx] = alpha * acc_ref[idx] + pv
    m_ref[idx] = m_new


def _fox_kernel(i_ref, j_ref, qt_ref, vt_ref, ka_ref, o_ref, m_ref, l_ref, acc_ref):
    p_id = pl.program_id(1)
    i = i_ref[p_id]
    j = j_ref[p_id]
    t = qt_ref.shape[1]

    @pl.when(j == 0)
    def _():
        m_ref[...] = jnp.full(m_ref.shape, NEG_BIG, jnp.float32)
        l_ref[...] = jnp.zeros(l_ref.shape, jnp.float32)
        acc_ref[...] = jnp.zeros(acc_ref.shape, jnp.float32)

    aug = jnp.where(lax.broadcasted_iota(jnp.int32, (HEAD_DIM, t), 0) < N_DECAY_PARTS,
                    1.0, 0.0).astype(jnp.bfloat16)

    def sweep(masked):
        def head(h, carry):
            r0 = pl.multiple_of(h * HEAD_DIM, HEAD_DIM)
            qa = jnp.concatenate([qt_ref[pl.ds(r0, HEAD_DIM), :], aug], axis=0)
            s = jnp.dot(ka_ref[0, h], qa, preferred_element_type=jnp.float32)
            if masked:
                kpos = lax.broadcasted_iota(jnp.int32, s.shape, 0)
                qpos = lax.broadcasted_iota(jnp.int32, s.shape, 1)
                s = jnp.where(kpos <= qpos, s, NEG_BIG)
            _online_update(s, vt_ref[pl.ds(r0, HEAD_DIM), :], m_ref, l_ref, acc_ref, h)
            return carry
        lax.fori_loop(0, FOX_HEADS, head, 0)

    @pl.when(j < i)
    def _():
        sweep(False)

    @pl.when(j == i)
    def _():
        sweep(True)
        for h in range(FOX_HEADS):
            o_ref[h * HEAD_DIM:(h + 1) * HEAD_DIM, :] = (
                acc_ref[h] / l_ref[h]).astype(o_ref.dtype)


def _fox_attn(zt, kaug, pairs_i, pairs_j, batch, seq, t):
    nt = seq // t
    n = batch * seq
    grid_spec = pltpu.PrefetchScalarGridSpec(
        num_scalar_prefetch=2,
        grid=(batch, pairs_i.shape[0]),
        in_specs=[
            pl.BlockSpec((FOX_WIDTH, t), lambda b, p, ii, jj: (0, b * nt + ii[p])),
            pl.BlockSpec((FOX_WIDTH, t), lambda b, p, ii, jj: (1, b * nt + jj[p])),
            pl.BlockSpec((1, FOX_HEADS, t, LANES), lambda b, p, ii, jj: (b, 0, jj[p], 0)),
        ],
        out_specs=pl.BlockSpec((FOX_WIDTH, t), lambda b, p, ii, jj: (0, b * nt + ii[p])),
        scratch_shapes=[
            pltpu.VMEM((FOX_HEADS, 1, t), jnp.float32),
            pltpu.VMEM((FOX_HEADS, 1, t), jnp.float32),
            pltpu.VMEM((FOX_HEADS, HEAD_DIM, t), jnp.float32),
        ],
    )
    return pl.pallas_call(
        _fox_kernel,
        grid_spec=grid_spec,
        out_shape=jax.ShapeDtypeStruct((FOX_WIDTH, n), jnp.bfloat16),
        compiler_params=pltpu.CompilerParams(
            dimension_semantics=("arbitrary", "arbitrary"), vmem_limit_bytes=VMEM_LIMIT),
        name="fox_attn",
    )(pairs_i, pairs_j, zt, zt, kaug)


def _diff_kernel(i_ref, j_ref, qt_ref, vt_ref, kd_ref, bd_ref, bs_ref, lam_ref, g_ref,
                 o_ref, m_ref, l_ref, acc_ref, *, lambda_init):
    p_id = pl.program_id(1)
    i = i_ref[p_id]
    j = j_ref[p_id]
    t = qt_ref.shape[1]

    @pl.when(j == 0)
    def _():
        m_ref[...] = jnp.full(m_ref.shape, NEG_BIG, jnp.float32)
        l_ref[...] = jnp.zeros(l_ref.shape, jnp.float32)
        acc_ref[...] = jnp.zeros(acc_ref.shape, jnp.float32)

    first_map = lax.broadcasted_iota(jnp.int32, (DIFF_V_DIM, t), 0) < HEAD_DIM

    def sweep(kind):
        def head(h, carry):
            r0 = pl.multiple_of(h * DIFF_V_DIM, DIFF_V_DIM)
            q = qt_ref[pl.ds(r0, DIFF_V_DIM), :]
            kd = kd_ref[0, h]
            v_t = vt_ref[pl.ds(r0, DIFF_V_DIM), :]
            zero = jnp.zeros_like(q)
            for mp in range(2):
                qm = jnp.where(first_map, q, zero) if mp == 0 else jnp.where(first_map, zero, q)
                if kind == "sub":
                    s_far = jnp.dot(kd[:t - LANES], qm, preferred_element_type=jnp.float32)
                    s_near = jnp.dot(kd[t - LANES:], qm, preferred_element_type=jnp.float32)
                    s = jnp.concatenate([s_far, s_near + bs_ref[h]], axis=0)
                else:
                    s = jnp.dot(kd, qm, preferred_element_type=jnp.float32)
                    if kind == "diag":
                        s = s + bd_ref[h]
                _online_update(s, v_t, m_ref, l_ref, acc_ref, 2 * h + mp)
            return carry
        lax.fori_loop(0, DIFF_HEADS, head, 0)

    @pl.when(j < i - 1)
    def _():
        sweep("far")

    @pl.when(j == i - 1)
    def _():
        sweep("sub")

    @pl.when(j == i)
    def _():
        sweep("diag")
        lam_rows = lam_ref[...]
        dot1 = jnp.sum(lam_rows[0:1] * lam_rows[1:2], axis=-1, keepdims=True)
        dot2 = jnp.sum(lam_rows[2:3] * lam_rows[3:4], axis=-1, keepdims=True)
        lam = jnp.exp(dot1) - jnp.exp(dot2) + lambda_init
        g = g_ref[...] * (1.0 - lambda_init)
        for h in range(DIFF_HEADS):
            d = acc_ref[2 * h] / l_ref[2 * h] - lam * (acc_ref[2 * h + 1] / l_ref[2 * h + 1])
            y = d * lax.rsqrt(jnp.mean(d * d, axis=0, keepdims=True) + SUBLN_EPS)
            o_ref[h * DIFF_V_DIM:(h + 1) * DIFF_V_DIM, :] = (y * g).astype(o_ref.dtype)


def _diff_attn(zt, kd, bias_diag, bias_sub, lam_rows, g_col, pairs_i, pairs_j,
               batch, seq, t, lambda_init):
    nt = seq // t
    n = batch * seq
    grid_spec = pltpu.PrefetchScalarGridSpec(
        num_scalar_prefetch=2,
        grid=(batch, pairs_i.shape[0]),
        in_specs=[
            pl.BlockSpec((DIFF_WIDTH, t), lambda b, p, ii, jj: (2, b * nt + ii[p])),
            pl.BlockSpec((DIFF_WIDTH, t), lambda b, p, ii, jj: (3, b * nt + jj[p])),
            pl.BlockSpec((1, DIFF_HEADS, t, LANES), lambda b, p, ii, jj: (b, 0, jj[p], 0)),
            pl.BlockSpec((DIFF_HEADS, t, t), lambda b, p, ii, jj: (0, 0, 0)),
            pl.BlockSpec((DIFF_HEADS, LANES, t), lambda b, p, ii, jj: (0, 0, 0)),
            pl.BlockSpec((8, LANES), lambda b, p, ii, jj: (0, 0)),
            pl.BlockSpec((DIFF_V_DIM, 1), lambda b, p, ii, jj: (0, 0)),
        ],
        out_specs=pl.BlockSpec((DIFF_WIDTH, t), lambda b, p, ii, jj: (0, b * nt + ii[p])),
        scratch_shapes=[
            pltpu.VMEM((2 * DIFF_HEADS, 1, t), jnp.float32),
            pltpu.VMEM((2 * DIFF_HEADS, 1, t), jnp.float32),
            pltpu.VMEM((2 * DIFF_HEADS, DIFF_V_DIM, t), jnp.float32),
        ],
    )
    return pl.pallas_call(
        functools.partial(_diff_kernel, lambda_init=lambda_init),
        grid_spec=grid_spec,
        out_shape=jax.ShapeDtypeStruct((DIFF_WIDTH, n), jnp.bfloat16),
        compiler_params=pltpu.CompilerParams(
            dimension_semantics=("arbitrary", "arbitrary"), vmem_limit_bytes=VMEM_LIMIT),
        name="diff_attn",
    )(pairs_i, pairs_j, zt, zt, kd, bias_diag, bias_sub, lam_rows, g_col)


def _out_mlp_kernel(x_ref, fox_ref, dif_ref, wo_ref, g_ref, w1_ref, w2_ref, gf_ref, o_ref,
                    *, final_norm):
    y = lax.dot_general(fox_ref[...], wo_ref[:FOX_WIDTH, :], _TN,
                        preferred_element_type=jnp.float32)
    y = y + lax.dot_general(dif_ref[...], wo_ref[FOX_WIDTH:, :], _TN,
                            preferred_element_type=jnp.float32)
    x1 = x_ref[...] + y
    h = _rms(x1, g_ref[...], NORM_EPS).astype(jnp.bfloat16)
    acc = x1
    for c in range(D_FF // FF_CHUNK):
        u = jnp.dot(h, w1_ref[:, c * FF_CHUNK:(c + 1) * FF_CHUNK],
                    preferred_element_type=jnp.float32)
        u = jnp.square(jnp.maximum(u, 0.0)).astype(jnp.bfloat16)
        acc = acc + jnp.dot(u, w2_ref[c * FF_CHUNK:(c + 1) * FF_CHUNK, :],
                            preferred_element_type=jnp.float32)
    if final_norm:
        acc = _rms(acc, gf_ref[...], NORM_EPS)
    o_ref[...] = acc


def _out_mlp(x2d, fox_t, dif_t, wo, g, w1, w2, gf, tm, final_norm):
    n = x2d.shape[0]
    const = lambda i: (0, 0)
    return pl.pallas_call(
        functools.partial(_out_mlp_kernel, final_norm=final_norm),
        grid=(n // tm,),
        in_specs=[
            pl.BlockSpec((tm, D_MODEL), lambda i: (i, 0)),
            pl.BlockSpec((FOX_WIDTH, tm), lambda i: (0, i)),
            pl.BlockSpec((DIFF_WIDTH, tm), lambda i: (0, i)),
            pl.BlockSpec((D_MODEL, D_MODEL), const, pipeline_mode=pl.Buffered(1)),
            pl.BlockSpec((1, D_MODEL), const),
            pl.BlockSpec((D_MODEL, D_FF), const, pipeline_mode=pl.Buffered(1)),
            pl.BlockSpec((D_FF, D_MODEL), const, pipeline_mode=pl.Buffered(1)),
            pl.BlockSpec((1, D_MODEL), const),
        ],
        out_specs=pl.BlockSpec((tm, D_MODEL), lambda i: (i, 0)),
        out_shape=jax.ShapeDtypeStruct((n, D_MODEL), jnp.float32),
        compiler_params=pltpu.CompilerParams(
            dimension_semantics=("arbitrary",), vmem_limit_bytes=VMEM_LIMIT),
        name="out_mlp",
    )(x2d, fox_t, dif_t, wo, g, w1, w2, gf)


def _t5_bucket(rel):
    half = N_BUCKETS // 2
    max_exact = half // 2
    ret = jnp.where(rel > 0, half, 0)
    n = jnp.abs(rel)
    nf = jnp.maximum(n, 1).astype(jnp.float32)
    large = max_exact + (jnp.log(nf / max_exact) / math.log(MAX_DISTANCE / max_exact)
                         * (half - max_exact)).astype(jnp.int32)
    large = jnp.minimum(large, half - 1)
    return ret + jnp.where(n < max_exact, n, large)


def _bias_tiles(table, t):
    tab = table.astype(jnp.float32)
    far = tab[_t5_bucket(jnp.int32(-MAX_DISTANCE))]
    k = jnp.arange(t)[:, None]
    q = jnp.arange(t)[None, :]
    rel = k - q
    diag = (tab[_t5_bucket(rel)] - far) * LOG2E
    diag = jnp.where(((k // CHUNK) <= (q // CHUNK))[..., None], diag, NEG_BIG)
    rel_sub = rel[t - LANES:] - t
    sub = (tab[_t5_bucket(rel_sub)] - far) * LOG2E
    return jnp.moveaxis(diag, -1, 0), jnp.moveaxis(sub, -1, 0)


def _split_bf16(x, parts):
    out = []
    for _ in range(parts):
        piece = x.astype(jnp.bfloat16)
        out.append(piece)
        x = x - piece.astype(jnp.float32)
    return out


def _forward(x, w_in, b_f, lambda_q1, lambda_k1, lambda_q2, lambda_k2, subln_g, w_out,
             norm_attn_g, norm_mlp_g, w_mlp_in, w_mlp_out, rel_bias_table, final_norm_g,
             *, attn_tile, row_tile):
    batch, seq, _ = x.shape
    depth = w_in.shape[0]
    n = batch * seq
    t = min(attn_tile, seq)
    tm = min(row_tile, n)
    assert seq % t == 0 and n % tm == 0 and seq % LANES == 0 and t % LANES == 0 and t > LANES
    nt = seq // t
    pairs = [(i, j) for i in range(nt) for j in range(i + 1)]
    pairs_i = jnp.asarray([p[0] for p in pairs], jnp.int32)
    pairs_j = jnp.asarray([p[1] for p in pairs], jnp.int32)
    bias_diag, bias_sub = _bias_tiles(rel_bias_table, t)

    o_fq, o_fk, o_fv = 0, FOX_WIDTH, 2 * FOX_WIDTH
    o_ff = 3 * FOX_WIDTH
    o_dq = o_ff + FOX_HEADS
    o_dk = o_dq + DIFF_WIDTH
    o_dv = o_dk + DIFF_WIDTH
    qscale = LOG2E / math.sqrt(HEAD_DIM)

    x2d = x.reshape(n, D_MODEL)
    for l in range(depth):
        w = w_in[l]
        wqv_t = jnp.concatenate([
            w[:, o_fq:o_fq + FOX_WIDTH] * qscale, w[:, o_fv:o_fv + FOX_WIDTH],
            w[:, o_dq:o_dq + DIFF_WIDTH] * qscale, w[:, o_dv:o_dv + DIFF_WIDTH]],
            axis=1).T.astype(jnp.bfloat16)
        wk = jnp.concatenate([w[:, o_fk:o_fk + FOX_WIDTH], w[:, o_dk:o_dk + DIFF_WIDTH]],
                             axis=1).astype(jnp.bfloat16)
        wf = jnp.pad(w[:, o_ff:o_ff + FOX_HEADS], ((0, 0), (0, LANES - FOX_HEADS))
                     ).astype(jnp.bfloat16)
        bf = jnp.pad(b_f[l].astype(jnp.float32), (0, LANES - FOX_HEADS)).reshape(1, LANES)

        zt, zk, lf = _in_proj(x2d, norm_attn_g[l].reshape(1, D_MODEL), wqv_t, wk, wf, bf, tm)

        lf_rows = lf[:, :FOX_HEADS].reshape(batch, seq, FOX_HEADS).transpose(0, 2, 1)
        c = _decay_scan(lf_rows.reshape(batch * FOX_HEADS * seq // LANES, LANES), seq // LANES)
        c = c.reshape(batch, FOX_HEADS, seq)
        parts = _split_bf16(c * (-LOG2E), N_DECAY_PARTS)
        kf = zk[:, :FOX_WIDTH].reshape(batch, seq, FOX_HEADS, HEAD_DIM).transpose(0, 2, 1, 3)
        kaug = jnp.concatenate(
            [kf] + [p[..., None] for p in parts]
            + [jnp.zeros((batch, FOX_HEADS, seq, LANES - HEAD_DIM - N_DECAY_PARTS), jnp.bfloat16)],
            axis=-1)
        kd = zk[:, FOX_WIDTH:].reshape(batch, seq, DIFF_HEADS, DIFF_V_DIM).transpose(0, 2, 1, 3)

        fox_t = _fox_attn(zt, kaug, pairs_i, pairs_j, batch, seq, t)

        lambda_init = 0.8 - 0.6 * math.exp(-0.3 * l)
        lam_rows = jnp.pad(
            jnp.stack([lambda_q1[l], lambda_k1[l], lambda_q2[l], lambda_k2[l]]).astype(jnp.float32),
            ((0, 4), (0, LANES - HEAD_DIM)))
        dif_t = _diff_attn(zt, kd, bias_diag, bias_sub, lam_rows,
                           subln_g[l].astype(jnp.float32).reshape(DIFF_V_DIM, 1),
                           pairs_i, pairs_j, batch, seq, t, lambda_init)

        x2d = _out_mlp(x2d, fox_t, dif_t, w_out[l].astype(jnp.bfloat16),
                       norm_mlp_g[l].reshape(1, D_MODEL), w_mlp_in[l].astype(jnp.bfloat16),
                       w_mlp_out[l].astype(jnp.bfloat16), final_norm_g.reshape(1, D_MODEL),
                       tm, final_norm=(l == depth - 1))
    return x2d.reshape(batch, seq, D_MODEL)


def kernel(x, w_in, b_f, lambda_q1, lambda_k1, lambda_q2, lambda_k2, subln_g, w_out,
           norm_attn_g, norm_mlp_g, w_mlp_in, w_mlp_out, rel_bias_table, final_norm_g):
    return _forward(x, w_in, b_f, lambda_q1, lambda_k1, lambda_q2, lambda_k2, subln_g, w_out,
                    norm_attn_g, norm_mlp_g, w_mlp_in, w_mlp_out, rel_bias_table, final_norm_g,
                    attn_tile=ATTN_TILE, row_tile=ROW_TILE)
```

```python
import functools
import math

import jax
import jax.numpy as jnp
import numpy as np
from jax import lax
from jax.experimental import pallas as pl
from jax.experimental.pallas import tpu as pltpu

D_MODEL = 1024
CHUNK = 64
FOX_HEADS = 8
HEAD_DIM = 64
FOX_WIDTH = FOX_HEADS * HEAD_DIM
DIFF_HEADS = 4
DIFF_V_DIM = 2 * HEAD_DIM
DIFF_WIDTH = DIFF_HEADS * DIFF_V_DIM
D_FF = 4 * D_MODEL
N_BUCKETS = 32
MAX_DISTANCE = 128
NORM_EPS = 1e-5
SUBLN_EPS = 1e-5

LANES = 128
LOG2E = math.log2(math.e)
NEG_BIG = -1e30
N_DECAY_PARTS = 3
SUM_ROWS = 16
ATTN_TILE = 512
ROW_TILE = 512
FF_CHUNK = 1024
VMEM_LIMIT = 48 * 1024 * 1024

_NT = (((1,), (1,)), ((), ()))
_TN = (((0,), (0,)), ((), ()))


def _rms(x, g, eps):
    return x * lax.rsqrt(jnp.mean(x * x, axis=-1, keepdims=True) + eps) * g


def _in_proj_kernel(x_ref, g_ref, wqv_ref, wk_ref, wf_ref, bf_ref, zt_ref, zk_ref, lf_ref):
    h = _rms(x_ref[...], g_ref[...], NORM_EPS).astype(jnp.bfloat16)
    zt = lax.dot_general(wqv_ref[...], h, _NT, preferred_element_type=jnp.float32)
    zt_ref[...] = zt.astype(jnp.bfloat16)
    zk_ref[...] = jnp.dot(h, wk_ref[...], preferred_element_type=jnp.float32).astype(jnp.bfloat16)
    f = jnp.dot(h, wf_ref[...], preferred_element_type=jnp.float32) + bf_ref[...]
    lf_ref[...] = jnp.minimum(f, 0.0) - jnp.log1p(jnp.exp(-jnp.abs(f)))


def _in_proj(x2d, g, wqv_t, wk, wf, bf, tm):
    n = x2d.shape[0]
    nqv = wqv_t.shape[0]
    nk = wk.shape[1]
    const = lambda i: (0, 0)
    return pl.pallas_call(
        _in_proj_kernel,
        grid=(n // tm,),
        in_specs=[
            pl.BlockSpec((tm, D_MODEL), lambda i: (i, 0)),
            pl.BlockSpec((1, D_MODEL), const),
            pl.BlockSpec((nqv, D_MODEL), const),
            pl.BlockSpec((D_MODEL, nk), const),
            pl.BlockSpec((D_MODEL, LANES), const),
            pl.BlockSpec((1, LANES), const),
        ],
        out_specs=[
            pl.BlockSpec((nqv, tm), lambda i: (0, i)),
            pl.BlockSpec((tm, nk), lambda i: (i, 0)),
            pl.BlockSpec((tm, LANES), lambda i: (i, 0)),
        ],
        out_shape=[
            jax.ShapeDtypeStruct((nqv, n), jnp.bfloat16),
            jax.ShapeDtypeStruct((n, nk), jnp.bfloat16),
            jax.ShapeDtypeStruct((n, LANES), jnp.float32),
        ],
        compiler_params=pltpu.CompilerParams(
            dimension_semantics=("arbitrary",), vmem_limit_bytes=VMEM_LIMIT),
        name="in_proj",
    )(x2d, g, wqv_t, wk, wf, bf)


def _decay_scan_kernel(x_ref, o_ref, *, rows_per_seq):
    x = x_ref[...]
    lane = lax.broadcasted_iota(jnp.int32, x.shape, 1)
    d = 1
    while d < LANES:
        x = x + jnp.where(lane >= d, pltpu.roll(x, d, 1), 0.0)
        d *= 2
    tot = jnp.broadcast_to(x[:, LANES - 1:LANES], x.shape)
    row = lax.broadcasted_iota(jnp.int32, x.shape, 0) % rows_per_seq
    inc = tot
    d = 1
    while d < rows_per_seq:
        inc = inc + jnp.where(row >= d, pltpu.roll(inc, d, 0), 0.0)
        d *= 2
    o_ref[...] = x + (inc - tot)


def _decay_scan(lf_rows, rows_per_seq):
    return pl.pallas_call(
        functools.partial(_decay_scan_kernel, rows_per_seq=rows_per_seq),
        out_shape=jax.ShapeDtypeStruct(lf_rows.shape, jnp.float32),
        compiler_params=pltpu.CompilerParams(vmem_limit_bytes=VMEM_LIMIT),
        name="decay_scan",
    )(lf_rows)


def _ones_rows(t):
    return jnp.where(lax.broadcasted_iota(jnp.int32, (SUM_ROWS, t), 0) == 0, 1.0, 0.0
                     ).astype(jnp.bfloat16)


def _flash_units(n_units, score_fn, value_fn, s_bufs, m_ref, acc_ref):
    def produce(u):
        s = score_fn(u)
        s_bufs[u % 2][...] = s
        return jnp.max(s, axis=0, keepdims=True)

    mx = produce(0)
    for u in range(n_units):
        mx_next = produce(u + 1) if u + 1 < n_units else None
        m_prev = m_ref[u]
        m_new = jnp.maximum(m_prev, mx)
        m_ref[u] = m_new
        p = jnp.exp2(s_bufs[u % 2][...] - m_new).astype(jnp.bfloat16)
        pv = jnp.dot(value_fn(u), p, preferred_element_type=jnp.float32)
        acc_ref[u] = jnp.exp2(m_prev - m_new) * acc_ref[u] + pv
        mx = mx_next


def _init_stats(m_ref, acc_ref):
    m_ref[...] = jnp.full(m_ref.shape, NEG_BIG, jnp.float32)
    acc_ref[...] = jnp.zeros(acc_ref.shape, jnp.float32)


def _fox_kernel(i_ref, j_ref, qt_ref, vt_ref, ka_ref, o_ref, s0_ref, s1_ref, m_ref, acc_ref):
    p_id = pl.program_id(1)
    i = i_ref[p_id]
    j = j_ref[p_id]
    t = qt_ref.shape[1]

    @pl.when(j == 0)
    def _():
        _init_stats(m_ref, acc_ref)

    aug = jnp.where(lax.broadcasted_iota(jnp.int32, (HEAD_DIM, t), 0) < N_DECAY_PARTS,
                    1.0, 0.0).astype(jnp.bfloat16)
    ones = _ones_rows(t)

    def rows(h):
        return slice(h * HEAD_DIM, (h + 1) * HEAD_DIM)

    def sweep(masked):
        def scores(h):
            qa = jnp.concatenate([qt_ref[rows(h), :], aug], axis=0)
            s = jnp.dot(ka_ref[0, h], qa, preferred_element_type=jnp.float32)
            if masked:
                kpos = lax.broadcasted_iota(jnp.int32, s.shape, 0)
                qpos = lax.broadcasted_iota(jnp.int32, s.shape, 1)
                s = jnp.where(kpos <= qpos, s, NEG_BIG)
            return s

        def values(h):
            return jnp.concatenate([vt_ref[rows(h), :], ones], axis=0)

        _flash_units(FOX_HEADS, scores, values, (s0_ref, s1_ref), m_ref, acc_ref)

    @pl.when(j < i)
    def _():
        sweep(False)

    @pl.when(j == i)
    def _():
        sweep(True)
        for h in range(FOX_HEADS):
            a = acc_ref[h]
            o_ref[rows(h), :] = (a[:HEAD_DIM] / a[HEAD_DIM:HEAD_DIM + 1]).astype(o_ref.dtype)


def _fox_attn(zt, kaug, pairs_i, pairs_j, batch, seq, t):
    nt = seq // t
    n = batch * seq
    grid_spec = pltpu.PrefetchScalarGridSpec(
        num_scalar_prefetch=2,
        grid=(batch, pairs_i.shape[0]),
        in_specs=[
            pl.BlockSpec((FOX_WIDTH, t), lambda b, p, ii, jj: (0, b * nt + ii[p])),
            pl.BlockSpec((FOX_WIDTH, t), lambda b, p, ii, jj: (1, b * nt + jj[p])),
            pl.BlockSpec((1, FOX_HEADS, t, LANES), lambda b, p, ii, jj: (b, 0, jj[p], 0)),
        ],
        out_specs=pl.BlockSpec((FOX_WIDTH, t), lambda b, p, ii, jj: (0, b * nt + ii[p])),
        scratch_shapes=[
            pltpu.VMEM((t, t), jnp.float32),
            pltpu.VMEM((t, t), jnp.float32),
            pltpu.VMEM((FOX_HEADS, 1, t), jnp.float32),
            pltpu.VMEM((FOX_HEADS, HEAD_DIM + SUM_ROWS, t), jnp.float32),
        ],
    )
    return pl.pallas_call(
        _fox_kernel,
        grid_spec=grid_spec,
        out_shape=jax.ShapeDtypeStruct((FOX_WIDTH, n), jnp.bfloat16),
        compiler_params=pltpu.CompilerParams(
            dimension_semantics=("arbitrary", "arbitrary"), vmem_limit_bytes=VMEM_LIMIT),
        name="fox_attn",
    )(pairs_i, pairs_j, zt, zt, kaug)


def _diff_kernel(i_ref, j_ref, qt_ref, vt_ref, kd_ref, bd_ref, bs_ref, lam_ref, g_ref,
                 o_ref, s0_ref, s1_ref, m_ref, acc_ref, *, lambda_init):
    p_id = pl.program_id(1)
    i = i_ref[p_id]
    j = j_ref[p_id]
    t = qt_ref.shape[1]

    @pl.when(j == 0)
    def _():
        _init_stats(m_ref, acc_ref)

    first_map = lax.broadcasted_iota(jnp.int32, (DIFF_V_DIM, t), 0) < HEAD_DIM
    ones = _ones_rows(t)

    def rows(h):
        return slice(h * DIFF_V_DIM, (h + 1) * DIFF_V_DIM)

    def sweep(kind):
        def scores(u):
            h, mp = divmod(u, 2)
            q = qt_ref[rows(h), :]
            zero = jnp.zeros_like(q)
            qm = jnp.where(first_map, q, zero) if mp == 0 else jnp.where(first_map, zero, q)
            kd = kd_ref[0, h]
            if kind == "sub":
                s_far = jnp.dot(kd[:t - LANES], qm, preferred_element_type=jnp.float32)
                s_near = jnp.dot(kd[t - LANES:], qm, preferred_element_type=jnp.float32)
                return jnp.concatenate([s_far, s_near + bs_ref[h]], axis=0)
            s = jnp.dot(kd, qm, preferred_element_type=jnp.float32)
            if kind == "diag":
                s = s + bd_ref[h]
            return s

        def values(u):
            return jnp.concatenate([vt_ref[rows(u // 2), :], ones], axis=0)

        _flash_units(2 * DIFF_HEADS, scores, values, (s0_ref, s1_ref), m_ref, acc_ref)

    @pl.when(j < i - 1)
    def _():
        sweep("far")

    @pl.when(j == i - 1)
    def _():
        sweep("sub")

    @pl.when(j == i)
    def _():
        sweep("diag")
        lam_rows = lam_ref[...]
        dot1 = jnp.sum(lam_rows[0:1] * lam_rows[1:2], axis=-1, keepdims=True)
        dot2 = jnp.sum(lam_rows[2:3] * lam_rows[3:4], axis=-1, keepdims=True)
        lam = jnp.exp(dot1) - jnp.exp(dot2) + lambda_init
        g = g_ref[...] * (1.0 - lambda_init)
        for h in range(DIFF_HEADS):
            a1 = acc_ref[2 * h]
            a2 = acc_ref[2 * h + 1]
            d = (a1[:DIFF_V_DIM] / a1[DIFF_V_DIM:DIFF_V_DIM + 1]
                 - lam * (a2[:DIFF_V_DIM] / a2[DIFF_V_DIM:DIFF_V_DIM + 1]))
            y = d * lax.rsqrt(jnp.mean(d * d, axis=0, keepdims=True) + SUBLN_EPS)
            o_ref[rows(h), :] = (y * g).astype(o_ref.dtype)


def _diff_attn(zt, kd, bias_diag, bias_sub, lam_rows, g_col, pairs_i, pairs_j,
               batch, seq, t, lambda_init):
    nt = seq // t
    n = batch * seq
    grid_spec = pltpu.PrefetchScalarGridSpec(
        num_scalar_prefetch=2,
        grid=(batch, pairs_i.shape[0]),
        in_specs=[
            pl.BlockSpec((DIFF_WIDTH, t), lambda b, p, ii, jj: (2, b * nt + ii[p])),
            pl.BlockSpec((DIFF_WIDTH, t), lambda b, p, ii, jj: (3, b * nt + jj[p])),
            pl.BlockSpec((1, DIFF_HEADS, t, LANES), lambda b, p, ii, jj: (b, 0, jj[p], 0)),
            pl.BlockSpec((DIFF_HEADS, t, t), lambda b, p, ii, jj: (0, 0, 0)),
            pl.BlockSpec((DIFF_HEADS, LANES, t), lambda b, p, ii, jj: (0, 0, 0)),
            pl.BlockSpec((8, LANES), lambda b, p, ii, jj: (0, 0)),
            pl.BlockSpec((DIFF_V_DIM, 1), lambda b, p, ii, jj: (0, 0)),
        ],
        out_specs=pl.BlockSpec((DIFF_WIDTH, t), lambda b, p, ii, jj: (0, b * nt + ii[p])),
        scratch_shapes=[
            pltpu.VMEM((t, t), jnp.float32),
            pltpu.VMEM((t, t), jnp.float32),
            pltpu.VMEM((2 * DIFF_HEADS, 1, t), jnp.float32),
            pltpu.VMEM((2 * DIFF_HEADS, DIFF_V_DIM + SUM_ROWS, t), jnp.float32),
        ],
    )
    return pl.pallas_call(
        functools.partial(_diff_kernel, lambda_init=lambda_init),
        grid_spec=grid_spec,
        out_shape=jax.ShapeDtypeStruct((DIFF_WIDTH, n), jnp.bfloat16),
        compiler_params=pltpu.CompilerParams(
            dimension_semantics=("arbitrary", "arbitrary"), vmem_limit_bytes=VMEM_LIMIT),
        name="diff_attn",
    )(pairs_i, pairs_j, zt, zt, kd, bias_diag, bias_sub, lam_rows, g_col)


def _out_mlp_kernel(x_ref, fox_ref, dif_ref, wo_ref, g_ref, w1_ref, w2_ref, gf_ref, o_ref,
                    *, final_norm):
    y = lax.dot_general(fox_ref[...], wo_ref[:FOX_WIDTH, :], _TN,
                        preferred_element_type=jnp.float32)
    y = y + lax.dot_general(dif_ref[...], wo_ref[FOX_WIDTH:, :], _TN,
                            preferred_element_type=jnp.float32)
    x1 = x_ref[...] + y
    h = _rms(x1, g_ref[...], NORM_EPS).astype(jnp.bfloat16)
    acc = x1
    for c in range(D_FF // FF_CHUNK):
        u = jnp.dot(h, w1_ref[:, c * FF_CHUNK:(c + 1) * FF_CHUNK],
                    preferred_element_type=jnp.float32)
        u = jnp.square(jnp.maximum(u, 0.0)).astype(jnp.bfloat16)
        acc = acc + jnp.dot(u, w2_ref[c * FF_CHUNK:(c + 1) * FF_CHUNK, :],
                            preferred_element_type=jnp.float32)
    if final_norm:
        acc = _rms(acc, gf_ref[...], NORM_EPS)
    o_ref[...] = acc


def _out_mlp(x2d, fox_t, dif_t, wo, g, w1, w2, gf, tm, final_norm):
    n = x2d.shape[0]
    const = lambda i: (0, 0)
    return pl.pallas_call(
        functools.partial(_out_mlp_kernel, final_norm=final_norm),
        grid=(n // tm,),
        in_specs=[
            pl.BlockSpec((tm, D_MODEL), lambda i: (i, 0)),
            pl.BlockSpec((FOX_WIDTH, tm), lambda i: (0, i)),
            pl.BlockSpec((DIFF_WIDTH, tm), lambda i: (0, i)),
            pl.BlockSpec((D_MODEL, D_MODEL), const, pipeline_mode=pl.Buffered(1)),
            pl.BlockSpec((1, D_MODEL), const),
            pl.BlockSpec((D_MODEL, D_FF), const, pipeline_mode=pl.Buffered(1)),
            pl.BlockSpec((D_FF, D_MODEL), const, pipeline_mode=pl.Buffered(1)),
            pl.BlockSpec((1, D_MODEL), const),
        ],
        out_specs=pl.BlockSpec((tm, D_MODEL), lambda i: (i, 0)),
        out_shape=jax.ShapeDtypeStruct((n, D_MODEL), jnp.float32),
        compiler_params=pltpu.CompilerParams(
            dimension_semantics=("arbitrary",), vmem_limit_bytes=VMEM_LIMIT),
        name="out_mlp",
    )(x2d, fox_t, dif_t, wo, g, w1, w2, gf)


def _t5_bucket(rel):
    half = N_BUCKETS // 2
    max_exact = half // 2
    ret = jnp.where(rel > 0, half, 0)
    n = jnp.abs(rel)
    nf = jnp.maximum(n, 1).astype(jnp.float32)
    large = max_exact + (jnp.log(nf / max_exact) / math.log(MAX_DISTANCE / max_exact)
                         * (half - max_exact)).astype(jnp.int32)
    large = jnp.minimum(large, half - 1)
    return ret + jnp.where(n < max_exact, n, large)


def _bias_tiles(table, t):
    tab = table.astype(jnp.float32)
    far = tab[_t5_bucket(jnp.int32(-MAX_DISTANCE))]
    k = jnp.arange(t)[:, None]
    q = jnp.arange(t)[None, :]
    rel = k - q
    diag = (tab[_t5_bucket(rel)] - far) * LOG2E
    diag = jnp.where(((k // CHUNK) <= (q // CHUNK))[..., None], diag, NEG_BIG)
    rel_sub = rel[t - LANES:] - t
    sub = (tab[_t5_bucket(rel_sub)] - far) * LOG2E
    return jnp.moveaxis(diag, -1, 0), jnp.moveaxis(sub, -1, 0)


def _split_bf16(x, parts):
    out = []
    for _ in range(parts):
        piece = x.astype(jnp.bfloat16)
        out.append(piece)
        x = x - piece.astype(jnp.float32)
    return out


def _forward(x, w_in, b_f, lambda_q1, lambda_k1, lambda_q2, lambda_k2, subln_g, w_out,
             norm_attn_g, norm_mlp_g, w_mlp_in, w_mlp_out, rel_bias_table, final_norm_g,
             *, attn_tile, row_tile):
    batch, seq, _ = x.shape
    depth = w_in.shape[0]
    n = batch * seq
    t = min(attn_tile, seq)
    tm = min(row_tile, n)
    assert seq % t == 0 and n % tm == 0 and seq % LANES == 0 and t % LANES == 0 and t > LANES
    nt = seq // t
    pairs = [(i, j) for i in range(nt) for j in range(i + 1)]
    pairs_i = jnp.asarray([p[0] for p in pairs], jnp.int32)
    pairs_j = jnp.asarray([p[1] for p in pairs], jnp.int32)
    bias_diag, bias_sub = _bias_tiles(rel_bias_table, t)

    o_fq, o_fk, o_fv = 0, FOX_WIDTH, 2 * FOX_WIDTH
    o_ff = 3 * FOX_WIDTH
    o_dq = o_ff + FOX_HEADS
    o_dk = o_dq + DIFF_WIDTH
    o_dv = o_dk + DIFF_WIDTH
    qscale = LOG2E / math.sqrt(HEAD_DIM)

    x2d = x.reshape(n, D_MODEL)
    for l in range(depth):
        w = w_in[l]
        wqv_t = jnp.concatenate([
            w[:, o_fq:o_fq + FOX_WIDTH] * qscale, w[:, o_fv:o_fv + FOX_WIDTH],
            w[:, o_dq:o_dq + DIFF_WIDTH] * qscale, w[:, o_dv:o_dv + DIFF_WIDTH]],
            axis=1).T.astype(jnp.bfloat16)
        wk = jnp.concatenate([w[:, o_fk:o_fk + FOX_WIDTH], w[:, o_dk:o_dk + DIFF_WIDTH]],
                             axis=1).astype(jnp.bfloat16)
        wf = jnp.pad(w[:, o_ff:o_ff + FOX_HEADS], ((0, 0), (0, LANES - FOX_HEADS))
                     ).astype(jnp.bfloat16)
        bf = jnp.pad(b_f[l].astype(jnp.float32), (0, LANES - FOX_HEADS)).reshape(1, LANES)

        zt, zk, lf = _in_proj(x2d, norm_attn_g[l].reshape(1, D_MODEL), wqv_t, wk, wf, bf, tm)

        lf_rows = lf[:, :FOX_HEADS].reshape(batch, seq, FOX_HEADS).transpose(0, 2, 1)
        c = _decay_scan(lf_rows.reshape(batch * FOX_HEADS * seq // LANES, LANES), seq // LANES)
        c = c.reshape(batch, FOX_HEADS, seq)
        parts = _split_bf16(c * (-LOG2E), N_DECAY_PARTS)
        kf = zk[:, :FOX_WIDTH].reshape(batch, seq, FOX_HEADS, HEAD_DIM).transpose(0, 2, 1, 3)
        kaug = jnp.concatenate(
            [kf] + [p[..., None] for p in parts]
            + [jnp.zeros((batch, FOX_HEADS, seq, LANES - HEAD_DIM - N_DECAY_PARTS), jnp.bfloat16)],
            axis=-1)
        kd = zk[:, FOX_WIDTH:].reshape(batch, seq, DIFF_HEADS, DIFF_V_DIM).transpose(0, 2, 1, 3)

        fox_t = _fox_attn(zt, kaug, pairs_i, pairs_j, batch, seq, t)

        lambda_init = 0.8 - 0.6 * math.exp(-0.3 * l)
        lam_rows = jnp.pad(
            jnp.stack([lambda_q1[l], lambda_k1[l], lambda_q2[l], lambda_k2[l]]).astype(jnp.float32),
            ((0, 4), (0, LANES - HEAD_DIM)))
        dif_t = _diff_attn(zt, kd, bias_diag, bias_sub, lam_rows,
                           subln_g[l].astype(jnp.float32).reshape(DIFF_V_DIM, 1),
                           pairs_i, pairs_j, batch, seq, t, lambda_init)

        x2d = _out_mlp(x2d, fox_t, dif_t, w_out[l].astype(jnp.bfloat16),
                       norm_mlp_g[l].reshape(1, D_MODEL), w_mlp_in[l].astype(jnp.bfloat16),
                       w_mlp_out[l].astype(jnp.bfloat16), final_norm_g.reshape(1, D_MODEL),
                       tm, final_norm=(l == depth - 1))
    return x2d.reshape(batch, seq, D_MODEL)


def kernel(x, w_in, b_f, lambda_q1, lambda_k1, lambda_q2, lambda_k2, subln_g, w_out,
           norm_attn_g, norm_mlp_g, w_mlp_in, w_mlp_out, rel_bias_table, final_norm_g):
    return _forward(x, w_in, b_f, lambda_q1, lambda_k1, lambda_q2, lambda_k2, subln_g, w_out,
                    norm_attn_g, norm_mlp_g, w_mlp_in, w_mlp_out, rel_bias_table, final_norm_g,
                    attn_tile=ATTN_TILE, row_tile=ROW_TILE)
```

```python
import functools
import math

import jax
import jax.numpy as jnp
import numpy as np
from jax import lax
from jax.experimental import pallas as pl
from jax.experimental.pallas import tpu as pltpu

D_MODEL = 1024
CHUNK = 64
FOX_HEADS = 8
HEAD_DIM = 64
FOX_WIDTH = FOX_HEADS * HEAD_DIM
DIFF_HEADS = 4
DIFF_V_DIM = 2 * HEAD_DIM
DIFF_WIDTH = DIFF_HEADS * DIFF_V_DIM
D_FF = 4 * D_MODEL
N_BUCKETS = 32
MAX_DISTANCE = 128
NORM_EPS = 1e-5
SUBLN_EPS = 1e-5

LANES = 128
LOG2E = math.log2(math.e)
NEG_BIG = -1e30
N_DECAY_PARTS = 3
SUM_ROWS = 16
ATTN_TILE = 512
ROW_TILE = 512
FF_CHUNK = 1024
VMEM_LIMIT = 48 * 1024 * 1024

_NT = (((1,), (1,)), ((), ()))
_TN = (((0,), (0,)), ((), ()))


def _rms(x, g, eps):
    return x * lax.rsqrt(jnp.mean(x * x, axis=-1, keepdims=True) + eps) * g


def _in_proj_kernel(x_ref, g_ref, wqv_ref, wk_ref, wf_ref, bf_ref, zt_ref, zk_ref, lf_ref):
    h = _rms(x_ref[...], g_ref[...], NORM_EPS).astype(jnp.bfloat16)
    zt = lax.dot_general(wqv_ref[...], h, _NT, preferred_element_type=jnp.float32)
    zt_ref[...] = zt.astype(jnp.bfloat16)
    zk_ref[...] = jnp.dot(h, wk_ref[...], preferred_element_type=jnp.float32).astype(jnp.bfloat16)
    f = jnp.dot(h, wf_ref[...], preferred_element_type=jnp.float32) + bf_ref[...]
    lf_ref[...] = jnp.minimum(f, 0.0) - jnp.log1p(jnp.exp(-jnp.abs(f)))


def _in_proj(x2d, g, wqv_t, wk, wf, bf, tm):
    n = x2d.shape[0]
    nqv = wqv_t.shape[0]
    nk = wk.shape[1]
    const = lambda i: (0, 0)
    return pl.pallas_call(
        _in_proj_kernel,
        grid=(n // tm,),
        in_specs=[
            pl.BlockSpec((tm, D_MODEL), lambda i: (i, 0)),
            pl.BlockSpec((1, D_MODEL), const),
            pl.BlockSpec((nqv, D_MODEL), const),
            pl.BlockSpec((D_MODEL, nk), const),
            pl.BlockSpec((D_MODEL, LANES), const),
            pl.BlockSpec((1, LANES), const),
        ],
        out_specs=[
            pl.BlockSpec((nqv, tm), lambda i: (0, i)),
            pl.BlockSpec((tm, nk), lambda i: (i, 0)),
            pl.BlockSpec((tm, LANES), lambda i: (i, 0)),
        ],
        out_shape=[
            jax.ShapeDtypeStruct((nqv, n), jnp.bfloat16),
            jax.ShapeDtypeStruct((n, nk), jnp.bfloat16),
            jax.ShapeDtypeStruct((n, LANES), jnp.float32),
        ],
        compiler_params=pltpu.CompilerParams(
            dimension_semantics=("arbitrary",), vmem_limit_bytes=VMEM_LIMIT),
        name="in_proj",
    )(x2d, g, wqv_t, wk, wf, bf)


def _decay_scan_kernel(x_ref, o_ref, *, rows_per_seq):
    x = x_ref[...]
    lane = lax.broadcasted_iota(jnp.int32, x.shape, 1)
    d = 1
    while d < LANES:
        x = x + jnp.where(lane >= d, pltpu.roll(x, d, 1), 0.0)
        d *= 2
    tot = jnp.broadcast_to(x[:, LANES - 1:LANES], x.shape)
    row = lax.broadcasted_iota(jnp.int32, x.shape, 0) % rows_per_seq
    inc = tot
    d = 1
    while d < rows_per_seq:
        inc = inc + jnp.where(row >= d, pltpu.roll(inc, d, 0), 0.0)
        d *= 2
    o_ref[...] = x + (inc - tot)


def _decay_scan(lf_rows, rows_per_seq):
    return pl.pallas_call(
        functools.partial(_decay_scan_kernel, rows_per_seq=rows_per_seq),
        out_shape=jax.ShapeDtypeStruct(lf_rows.shape, jnp.float32),
        compiler_params=pltpu.CompilerParams(vmem_limit_bytes=VMEM_LIMIT),
        name="decay_scan",
    )(lf_rows)


def _ones_rows(t):
    return jnp.where(lax.broadcasted_iota(jnp.int32, (SUM_ROWS, t), 0) == 0, 1.0, 0.0
                     ).astype(jnp.bfloat16)


def _flash_units(n_units, score_fn, value_fn, s_bufs, m_ref, acc_ref):
    def produce(u):
        s = score_fn(u)
        s_bufs[u % 2][...] = s
        return jnp.max(s, axis=0, keepdims=True)

    mx = produce(0)
    for u in range(n_units):
        mx_next = produce(u + 1) if u + 1 < n_units else None
        m_prev = m_ref[u]
        m_new = jnp.maximum(m_prev, mx)
        m_ref[u] = m_new
        p = jnp.exp2(s_bufs[u % 2][...] - m_new).astype(jnp.bfloat16)
        pv = jnp.dot(value_fn(u), p, preferred_element_type=jnp.float32)
        acc_ref[u] = jnp.exp2(m_prev - m_new) * acc_ref[u] + pv
        mx = mx_next


def _init_stats(m_ref, acc_ref):
    m_ref[...] = jnp.full(m_ref.shape, NEG_BIG, jnp.float32)
    acc_ref[...] = jnp.zeros(acc_ref.shape, jnp.float32)


def _fox_kernel(i_ref, j_ref, qt_ref, vt_ref, k_ref, dec_ref, o_ref, s0_ref, s1_ref, m_ref, acc_ref):
    p_id = pl.program_id(1)
    i = i_ref[p_id]
    j = j_ref[p_id]
    t = qt_ref.shape[1]

    @pl.when(j == 0)
    def _():
        _init_stats(m_ref, acc_ref)

    ones = _ones_rows(t)
    zeros = jnp.zeros((HEAD_DIM, t), jnp.bfloat16)
    dec_row = lax.broadcasted_iota(jnp.int32, (LANES, t), 0)

    def rows(h):
        return slice(h * HEAD_DIM, (h + 1) * HEAD_DIM)

    def sweep(masked):
        def scores(h):
            q = qt_ref[rows(h), :]
            pick = ((dec_row >= N_DECAY_PARTS * h) & (dec_row < N_DECAY_PARTS * (h + 1)))
            qa = jnp.concatenate(([q, zeros] if h % 2 == 0 else [zeros, q])
                                 + [jnp.where(pick, 1.0, 0.0).astype(jnp.bfloat16)], axis=0)
            pair = slice((h // 2) * LANES, (h // 2 + 1) * LANES)
            ka = jnp.concatenate([k_ref[:, pair], dec_ref[0]], axis=1)
            s = jnp.dot(ka, qa, preferred_element_type=jnp.float32)
            if masked:
                kpos = lax.broadcasted_iota(jnp.int32, s.shape, 0)
                qpos = lax.broadcasted_iota(jnp.int32, s.shape, 1)
                s = jnp.where(kpos <= qpos, s, NEG_BIG)
            return s

        def values(h):
            return jnp.concatenate([vt_ref[rows(h), :], ones], axis=0)

        _flash_units(FOX_HEADS, scores, values, (s0_ref, s1_ref), m_ref, acc_ref)

    @pl.when(j < i)
    def _():
        sweep(False)

    @pl.when(j == i)
    def _():
        sweep(True)
        for h in range(FOX_HEADS):
            a = acc_ref[h]
            o_ref[rows(h), :] = (a[:HEAD_DIM] / a[HEAD_DIM:HEAD_DIM + 1]).astype(o_ref.dtype)


def _fox_attn(zt, zk, dec, pairs_i, pairs_j, batch, seq, t):
    nt = seq // t
    n = batch * seq
    grid_spec = pltpu.PrefetchScalarGridSpec(
        num_scalar_prefetch=2,
        grid=(batch, pairs_i.shape[0]),
        in_specs=[
            pl.BlockSpec((FOX_WIDTH, t), lambda b, p, ii, jj: (0, b * nt + ii[p])),
            pl.BlockSpec((FOX_WIDTH, t), lambda b, p, ii, jj: (1, b * nt + jj[p])),
            pl.BlockSpec((t, FOX_WIDTH), lambda b, p, ii, jj: (b * nt + jj[p], 0)),
            pl.BlockSpec((1, t, LANES), lambda b, p, ii, jj: (b, jj[p], 0)),
        ],
        out_specs=pl.BlockSpec((FOX_WIDTH, t), lambda b, p, ii, jj: (0, b * nt + ii[p])),
        scratch_shapes=[
            pltpu.VMEM((t, t), jnp.float32),
            pltpu.VMEM((t, t), jnp.float32),
            pltpu.VMEM((FOX_HEADS, 1, t), jnp.float32),
            pltpu.VMEM((FOX_HEADS, HEAD_DIM + SUM_ROWS, t), jnp.float32),
        ],
    )
    return pl.pallas_call(
        _fox_kernel,
        grid_spec=grid_spec,
        out_shape=jax.ShapeDtypeStruct((FOX_WIDTH, n), jnp.bfloat16),
        compiler_params=pltpu.CompilerParams(
            dimension_semantics=("arbitrary", "arbitrary"), vmem_limit_bytes=VMEM_LIMIT),
        name="fox_attn",
    )(pairs_i, pairs_j, zt, zt, zk, dec)


def _diff_kernel(i_ref, j_ref, qt_ref, vt_ref, kd_ref, bd_ref, bs_ref, lam_ref, g_ref,
                 o_ref, s0_ref, s1_ref, m_ref, acc_ref, *, lambda_init):
    p_id = pl.program_id(1)
    i = i_ref[p_id]
    j = j_ref[p_id]
    t = qt_ref.shape[1]

    @pl.when(j == 0)
    def _():
        _init_stats(m_ref, acc_ref)

    first_map = lax.broadcasted_iota(jnp.int32, (DIFF_V_DIM, t), 0) < HEAD_DIM
    ones = _ones_rows(t)

    def rows(h):
        return slice(h * DIFF_V_DIM, (h + 1) * DIFF_V_DIM)

    def sweep(kind):
        def scores(u):
            h, mp = divmod(u, 2)
            q = qt_ref[rows(h), :]
            zero = jnp.zeros_like(q)
            qm = jnp.where(first_map, q, zero) if mp == 0 else jnp.where(first_map, zero, q)
            kd = kd_ref[:, rows(h)]
            if kind == "sub":
                s_far = jnp.dot(kd[:t - LANES], qm, preferred_element_type=jnp.float32)
                s_near = jnp.dot(kd[t - LANES:], qm, preferred_element_type=jnp.float32)
                return jnp.concatenate([s_far, s_near + bs_ref[h]], axis=0)
            s = jnp.dot(kd, qm, preferred_element_type=jnp.float32)
            if kind == "diag":
                s = s + bd_ref[h]
            return s

        def values(u):
            return jnp.concatenate([vt_ref[rows(u // 2), :], ones], axis=0)

        _flash_units(2 * DIFF_HEADS, scores, values, (s0_ref, s1_ref), m_ref, acc_ref)

    @pl.when(j < i - 1)
    def _():
        sweep("far")

    @pl.when(j == i - 1)
    def _():
        sweep("sub")

    @pl.when(j == i)
    def _():
        sweep("diag")
        lam_rows = lam_ref[...]
        dot1 = jnp.sum(lam_rows[0:1] * lam_rows[1:2], axis=-1, keepdims=True)
        dot2 = jnp.sum(lam_rows[2:3] * lam_rows[3:4], axis=-1, keepdims=True)
        lam = jnp.exp(dot1) - jnp.exp(dot2) + lambda_init
        g = g_ref[...] * (1.0 - lambda_init)
        for h in range(DIFF_HEADS):
            a1 = acc_ref[2 * h]
            a2 = acc_ref[2 * h + 1]
            d = (a1[:DIFF_V_DIM] / a1[DIFF_V_DIM:DIFF_V_DIM + 1]
                 - lam * (a2[:DIFF_V_DIM] / a2[DIFF_V_DIM:DIFF_V_DIM + 1]))
            y = d * lax.rsqrt(jnp.mean(d * d, axis=0, keepdims=True) + SUBLN_EPS)
            o_ref[rows(h), :] = (y * g).astype(o_ref.dtype)


def _diff_attn(zt, kd, bias_diag, bias_sub, lam_rows, g_col, pairs_i, pairs_j,
               batch, seq, t, lambda_init):
    nt = seq // t
    n = batch * seq
    grid_spec = pltpu.PrefetchScalarGridSpec(
        num_scalar_prefetch=2,
        grid=(batch, pairs_i.shape[0]),
        in_specs=[
            pl.BlockSpec((DIFF_WIDTH, t), lambda b, p, ii, jj: (2, b * nt + ii[p])),
            pl.BlockSpec((DIFF_WIDTH, t), lambda b, p, ii, jj: (3, b * nt + jj[p])),
            pl.BlockSpec((t, DIFF_WIDTH), lambda b, p, ii, jj: (b * nt + jj[p], 1)),
            pl.BlockSpec((DIFF_HEADS, t, t), lambda b, p, ii, jj: (0, 0, 0)),
            pl.BlockSpec((DIFF_HEADS, LANES, t), lambda b, p, ii, jj: (0, 0, 0)),
            pl.BlockSpec((8, LANES), lambda b, p, ii, jj: (0, 0)),
            pl.BlockSpec((DIFF_V_DIM, 1), lambda b, p, ii, jj: (0, 0)),
        ],
        out_specs=pl.BlockSpec((DIFF_WIDTH, t), lambda b, p, ii, jj: (0, b * nt + ii[p])),
        scratch_shapes=[
            pltpu.VMEM((t, t), jnp.float32),
            pltpu.VMEM((t, t), jnp.float32),
            pltpu.VMEM((2 * DIFF_HEADS, 1, t), jnp.float32),
            pltpu.VMEM((2 * DIFF_HEADS, DIFF_V_DIM + SUM_ROWS, t), jnp.float32),
        ],
    )
    return pl.pallas_call(
        functools.partial(_diff_kernel, lambda_init=lambda_init),
        grid_spec=grid_spec,
        out_shape=jax.ShapeDtypeStruct((DIFF_WIDTH, n), jnp.bfloat16),
        compiler_params=pltpu.CompilerParams(
            dimension_semantics=("arbitrary", "arbitrary"), vmem_limit_bytes=VMEM_LIMIT),
        name="diff_attn",
    )(pairs_i, pairs_j, zt, zt, kd, bias_diag, bias_sub, lam_rows, g_col)


def _out_mlp_kernel(x_ref, fox_ref, dif_ref, wo_ref, g_ref, w1_ref, w2_ref, gf_ref, o_ref,
                    *, final_norm):
    y = lax.dot_general(fox_ref[...], wo_ref[:FOX_WIDTH, :], _TN,
                        preferred_element_type=jnp.float32)
    y = y + lax.dot_general(dif_ref[...], wo_ref[FOX_WIDTH:, :], _TN,
                            preferred_element_type=jnp.float32)
    x1 = x_ref[...] + y
    h = _rms(x1, g_ref[...], NORM_EPS).astype(jnp.bfloat16)
    acc = x1
    for c in range(D_FF // FF_CHUNK):
        u = jnp.dot(h, w1_ref[:, c * FF_CHUNK:(c + 1) * FF_CHUNK],
                    preferred_element_type=jnp.float32)
        u = jnp.square(jnp.maximum(u, 0.0)).astype(jnp.bfloat16)
        acc = acc + jnp.dot(u, w2_ref[c * FF_CHUNK:(c + 1) * FF_CHUNK, :],
                            preferred_element_type=jnp.float32)
    if final_norm:
        acc = _rms(acc, gf_ref[...], NORM_EPS)
    o_ref[...] = acc


def _out_mlp(x2d, fox_t, dif_t, wo, g, w1, w2, gf, tm, final_norm):
    n = x2d.shape[0]
    const = lambda i: (0, 0)
    return pl.pallas_call(
        functools.partial(_out_mlp_kernel, final_norm=final_norm),
        grid=(n // tm,),
        in_specs=[
            pl.BlockSpec((tm, D_MODEL), lambda i: (i, 0)),
            pl.BlockSpec((FOX_WIDTH, tm), lambda i: (0, i)),
            pl.BlockSpec((DIFF_WIDTH, tm), lambda i: (0, i)),
            pl.BlockSpec((D_MODEL, D_MODEL), const, pipeline_mode=pl.Buffered(1)),
            pl.BlockSpec((1, D_MODEL), const),
            pl.BlockSpec((D_MODEL, D_FF), const, pipeline_mode=pl.Buffered(1)),
            pl.BlockSpec((D_FF, D_MODEL), const, pipeline_mode=pl.Buffered(1)),
            pl.BlockSpec((1, D_MODEL), const),
        ],
        out_specs=pl.BlockSpec((tm, D_MODEL), lambda i: (i, 0)),
        out_shape=jax.ShapeDtypeStruct((n, D_MODEL), jnp.float32),
        compiler_params=pltpu.CompilerParams(
            dimension_semantics=("arbitrary",), vmem_limit_bytes=VMEM_LIMIT),
        name="out_mlp",
    )(x2d, fox_t, dif_t, wo, g, w1, w2, gf)


def _t5_bucket(rel):
    half = N_BUCKETS // 2
    max_exact = half // 2
    ret = jnp.where(rel > 0, half, 0)
    n = jnp.abs(rel)
    nf = jnp.maximum(n, 1).astype(jnp.float32)
    large = max_exact + (jnp.log(nf / max_exact) / math.log(MAX_DISTANCE / max_exact)
                         * (half - max_exact)).astype(jnp.int32)
    large = jnp.minimum(large, half - 1)
    return ret + jnp.where(n < max_exact, n, large)


def _bias_tiles(table, t):
    tab = table.astype(jnp.float32)

    def lookup(rel):
        bucket = _t5_bucket(rel)
        out = jnp.zeros((tab.shape[1],) + rel.shape, jnp.float32)
        for b in range(N_BUCKETS):
            out = jnp.where(bucket == b, tab[b].reshape((-1,) + (1,) * rel.ndim), out)
        return out

    far = lookup(jnp.full((1, 1), -MAX_DISTANCE, jnp.int32))
    k = jnp.arange(t)[:, None]
    q = jnp.arange(t)[None, :]
    rel = k - q
    diag = (lookup(rel) - far) * LOG2E
    diag = jnp.where((k // CHUNK) <= (q // CHUNK), diag, NEG_BIG)
    rel_sub = rel[t - LANES:] - t
    sub = (lookup(rel_sub) - far) * LOG2E
    return diag, sub


def _split_bf16(x, parts):
    out = []
    for _ in range(parts - 1):
        bits = lax.bitcast_convert_type(x, jnp.uint32) & jnp.uint32(0xFFFF0000)
        piece = lax.bitcast_convert_type(bits, jnp.float32)
        out.append(piece.astype(jnp.bfloat16))
        x = x - piece
    out.append(x.astype(jnp.bfloat16))
    return out


def _forward(x, w_in, b_f, lambda_q1, lambda_k1, lambda_q2, lambda_k2, subln_g, w_out,
             norm_attn_g, norm_mlp_g, w_mlp_in, w_mlp_out, rel_bias_table, final_norm_g,
             *, attn_tile, row_tile):
    batch, seq, _ = x.shape
    depth = w_in.shape[0]
    n = batch * seq
    t = min(attn_tile, seq)
    tm = min(row_tile, n)
    assert seq % t == 0 and n % tm == 0 and seq % LANES == 0 and t % LANES == 0 and t > LANES
    nt = seq // t
    pairs = [(i, j) for i in range(nt) for j in range(i + 1)]
    pairs_i = jnp.asarray([p[0] for p in pairs], jnp.int32)
    pairs_j = jnp.asarray([p[1] for p in pairs], jnp.int32)
    bias_diag, bias_sub = _bias_tiles(rel_bias_table, t)

    o_fq, o_fk, o_fv = 0, FOX_WIDTH, 2 * FOX_WIDTH
    o_ff = 3 * FOX_WIDTH
    o_dq = o_ff + FOX_HEADS
    o_dk = o_dq + DIFF_WIDTH
    o_dv = o_dk + DIFF_WIDTH
    qscale = LOG2E / math.sqrt(HEAD_DIM)

    x2d = x.reshape(n, D_MODEL)
    for l in range(depth):
        w = w_in[l]
        wqv_t = jnp.concatenate([
            w[:, o_fq:o_fq + FOX_WIDTH] * qscale, w[:, o_fv:o_fv + FOX_WIDTH],
            w[:, o_dq:o_dq + DIFF_WIDTH] * qscale, w[:, o_dv:o_dv + DIFF_WIDTH]],
            axis=1).T.astype(jnp.bfloat16)
        wk = jnp.concatenate([w[:, o_fk:o_fk + FOX_WIDTH], w[:, o_dk:o_dk + DIFF_WIDTH]],
                             axis=1).astype(jnp.bfloat16)
        wf = jnp.pad(w[:, o_ff:o_ff + FOX_HEADS], ((0, 0), (0, LANES - FOX_HEADS))
                     ).astype(jnp.bfloat16)
        bf = jnp.pad(b_f[l].astype(jnp.float32), (0, LANES - FOX_HEADS)).reshape(1, LANES)

        zt, zk, lf = _in_proj(x2d, norm_attn_g[l].reshape(1, D_MODEL), wqv_t, wk, wf, bf, tm)

        lf_rows = lf[:, :FOX_HEADS].reshape(batch, seq, FOX_HEADS).transpose(0, 2, 1)
        c = _decay_scan(lf_rows.reshape(batch * FOX_HEADS * seq // LANES, LANES), seq // LANES)
        c = c.reshape(batch, FOX_HEADS, seq)
        parts = _split_bf16(c * (-LOG2E), N_DECAY_PARTS)
        dec = jnp.stack(parts, axis=2).transpose(0, 3, 1, 2)
        dec = jnp.pad(dec.reshape(batch, seq, FOX_HEADS * N_DECAY_PARTS),
                      ((0, 0), (0, 0), (0, LANES - FOX_HEADS * N_DECAY_PARTS)))

        fox_t = _fox_attn(zt, zk, dec, pairs_i, pairs_j, batch, seq, t)

        lambda_init = 0.8 - 0.6 * math.exp(-0.3 * l)
        lam_rows = jnp.pad(
            jnp.stack([lambda_q1[l], lambda_k1[l], lambda_q2[l], lambda_k2[l]]).astype(jnp.float32),
            ((0, 4), (0, LANES - HEAD_DIM)))
        dif_t = _diff_attn(zt, zk, bias_diag, bias_sub, lam_rows,
                           subln_g[l].astype(jnp.float32).reshape(DIFF_V_DIM, 1),
                           pairs_i, pairs_j, batch, seq, t, lambda_init)

        x2d = _out_mlp(x2d, fox_t, dif_t, w_out[l].astype(jnp.bfloat16),
                       norm_mlp_g[l].reshape(1, D_MODEL), w_mlp_in[l].astype(jnp.bfloat16),
                       w_mlp_out[l].astype(jnp.bfloat16), final_norm_g.reshape(1, D_MODEL),
                       tm, final_norm=(l == depth - 1))
    return x2d.reshape(batch, seq, D_MODEL)


def kernel(x, w_in, b_f, lambda_q1, lambda_k1, lambda_q2, lambda_k2, subln_g, w_out,
           norm_attn_g, norm_mlp_g, w_mlp_in, w_mlp_out, rel_bias_table, final_norm_g):
    return _forward(x, w_in, b_f, lambda_q1, lambda_k1, lambda_q2, lambda_k2, subln_g, w_out,
                    norm_attn_g, norm_mlp_g, w_mlp_in, w_mlp_out, rel_bias_table, final_norm_g,
                    attn_tile=ATTN_TILE, row_tile=ROW_TILE)
```

```python
import functools
import math

import jax
import jax.numpy as jnp
import numpy as np
from jax import lax
from jax.experimental import pallas as pl
from jax.experimental.pallas import tpu as pltpu

D_MODEL = 1024
CHUNK = 64
FOX_HEADS = 8
HEAD_DIM = 64
FOX_WIDTH = FOX_HEADS * HEAD_DIM
DIFF_HEADS = 4
DIFF_V_DIM = 2 * HEAD_DIM
DIFF_WIDTH = DIFF_HEADS * DIFF_V_DIM
D_FF = 4 * D_MODEL
N_BUCKETS = 32
MAX_DISTANCE = 128
NORM_EPS = 1e-5
SUBLN_EPS = 1e-5

LANES = 128
LOG2E = math.log2(math.e)
NEG_BIG = -1e30
N_DECAY_PARTS = 3
SUM_ROWS = 16
DEAD_GAP = 170.0
NORM_SLACK = 1.001
ATTN_TILE = 512
ROW_TILE = 512
FF_CHUNK = 1024
VMEM_LIMIT = 48 * 1024 * 1024

_NT = (((1,), (1,)), ((), ()))
_TN = (((0,), (0,)), ((), ()))


def _rms(x, g, eps):
    return x * lax.rsqrt(jnp.mean(x * x, axis=-1, keepdims=True) + eps) * g


def _in_proj_kernel(x_ref, g_ref, wqv_ref, wk_ref, wf_ref, bf_ref, zt_ref, zk_ref, lf_ref):
    h = _rms(x_ref[...], g_ref[...], NORM_EPS).astype(jnp.bfloat16)
    zt = lax.dot_general(wqv_ref[...], h, _NT, preferred_element_type=jnp.float32)
    zt_ref[...] = zt.astype(jnp.bfloat16)
    zk_ref[...] = jnp.dot(h, wk_ref[...], preferred_element_type=jnp.float32).astype(jnp.bfloat16)
    f = jnp.dot(h, wf_ref[...], preferred_element_type=jnp.float32) + bf_ref[...]
    lf_ref[...] = jnp.minimum(f, 0.0) - jnp.log1p(jnp.exp(-jnp.abs(f)))


def _in_proj(x2d, g, wqv_t, wk, wf, bf, tm):
    n = x2d.shape[0]
    nqv = wqv_t.shape[0]
    nk = wk.shape[1]
    const = lambda i: (0, 0)
    return pl.pallas_call(
        _in_proj_kernel,
        grid=(n // tm,),
        in_specs=[
            pl.BlockSpec((tm, D_MODEL), lambda i: (i, 0)),
            pl.BlockSpec((1, D_MODEL), const),
            pl.BlockSpec((nqv, D_MODEL), const),
            pl.BlockSpec((D_MODEL, nk), const),
            pl.BlockSpec((D_MODEL, LANES), const),
            pl.BlockSpec((1, LANES), const),
        ],
        out_specs=[
            pl.BlockSpec((nqv, tm), lambda i: (0, i)),
            pl.BlockSpec((tm, nk), lambda i: (i, 0)),
            pl.BlockSpec((tm, LANES), lambda i: (i, 0)),
        ],
        out_shape=[
            jax.ShapeDtypeStruct((nqv, n), jnp.bfloat16),
            jax.ShapeDtypeStruct((n, nk), jnp.bfloat16),
            jax.ShapeDtypeStruct((n, LANES), jnp.float32),
        ],
        compiler_params=pltpu.CompilerParams(
            dimension_semantics=("arbitrary",), vmem_limit_bytes=VMEM_LIMIT),
        name="in_proj",
    )(x2d, g, wqv_t, wk, wf, bf)


def _decay_scan_kernel(x_ref, o_ref, *, rows_per_seq):
    x = x_ref[...]
    lane = lax.broadcasted_iota(jnp.int32, x.shape, 1)
    d = 1
    while d < LANES:
        x = x + jnp.where(lane >= d, pltpu.roll(x, d, 1), 0.0)
        d *= 2
    tot = jnp.broadcast_to(x[:, LANES - 1:LANES], x.shape)
    row = lax.broadcasted_iota(jnp.int32, x.shape, 0) % rows_per_seq
    inc = tot
    d = 1
    while d < rows_per_seq:
        inc = inc + jnp.where(row >= d, pltpu.roll(inc, d, 0), 0.0)
        d *= 2
    o_ref[...] = x + (inc - tot)


def _decay_scan(lf_rows, rows_per_seq):
    return pl.pallas_call(
        functools.partial(_decay_scan_kernel, rows_per_seq=rows_per_seq),
        out_shape=jax.ShapeDtypeStruct(lf_rows.shape, jnp.float32),
        compiler_params=pltpu.CompilerParams(vmem_limit_bytes=VMEM_LIMIT),
        name="decay_scan",
    )(lf_rows)


def _ones_rows(t):
    return jnp.where(lax.broadcasted_iota(jnp.int32, (SUM_ROWS, t), 0) == 0, 1.0, 0.0
                     ).astype(jnp.bfloat16)


def _flash_units(n_units, score_fn, value_fn, s_bufs, m_ref, acc_ref):
    def produce(u):
        s = score_fn(u)
        s_bufs[u % 2][...] = s
        return jnp.max(s, axis=0, keepdims=True)

    mx = produce(0)
    for u in range(n_units):
        mx_next = produce(u + 1) if u + 1 < n_units else None
        m_prev = m_ref[u]
        m_new = jnp.maximum(m_prev, mx)
        m_ref[u] = m_new
        p = jnp.exp2(s_bufs[u % 2][...] - m_new).astype(jnp.bfloat16)
        pv = jnp.dot(value_fn(u), p, preferred_element_type=jnp.float32)
        acc_ref[u] = jnp.exp2(m_prev - m_new) * acc_ref[u] + pv
        mx = mx_next


def _init_stats(m_ref, acc_ref):
    m_ref[...] = jnp.full(m_ref.shape, NEG_BIG, jnp.float32)
    acc_ref[...] = jnp.zeros(acc_ref.shape, jnp.float32)


def _fox_kernel(i_ref, j_ref, j0_ref, qt_ref, vt_ref, k_ref, dec_ref, o_ref,
                s0_ref, s1_ref, m_ref, acc_ref):
    p_id = pl.program_id(1)
    i = i_ref[p_id]
    j = j_ref[p_id]
    j0 = j0_ref[pl.program_id(0) * pl.num_programs(1) + p_id]
    t = qt_ref.shape[1]

    @pl.when(j == i)
    def _():
        _init_stats(m_ref, acc_ref)

    ones = _ones_rows(t)
    zeros = jnp.zeros((HEAD_DIM, t), jnp.bfloat16)
    dec_row = lax.broadcasted_iota(jnp.int32, (LANES, t), 0)

    def rows(h):
        return slice(h * HEAD_DIM, (h + 1) * HEAD_DIM)

    def sweep(masked):
        def scores(h):
            q = qt_ref[rows(h), :]
            pick = ((dec_row >= N_DECAY_PARTS * h) & (dec_row < N_DECAY_PARTS * (h + 1)))
            qa = jnp.concatenate(([q, zeros] if h % 2 == 0 else [zeros, q])
                                 + [jnp.where(pick, 1.0, 0.0).astype(jnp.bfloat16)], axis=0)
            pair = slice((h // 2) * LANES, (h // 2 + 1) * LANES)
            ka = jnp.concatenate([k_ref[:, pair], dec_ref[0]], axis=1)
            s = jnp.dot(ka, qa, preferred_element_type=jnp.float32)
            if masked:
                kpos = lax.broadcasted_iota(jnp.int32, s.shape, 0)
                qpos = lax.broadcasted_iota(jnp.int32, s.shape, 1)
                s = jnp.where(kpos <= qpos, s, NEG_BIG)
            return s

        def values(h):
            return jnp.concatenate([vt_ref[rows(h), :], ones], axis=0)

        _flash_units(FOX_HEADS, scores, values, (s0_ref, s1_ref), m_ref, acc_ref)

    @pl.when(j == i)
    def _():
        sweep(True)

    @pl.when((j < i) & (j >= j0))
    def _():
        sweep(False)

    @pl.when(j == 0)
    def _():
        for h in range(FOX_HEADS):
            a = acc_ref[h]
            o_ref[rows(h), :] = (a[:HEAD_DIM] / a[HEAD_DIM:HEAD_DIM + 1]).astype(o_ref.dtype)


def _fox_attn(zt, zk, dec, pairs_i, pairs_j, first_tile, batch, seq, t):
    nt = seq // t
    n = batch * seq
    n_pairs = pairs_i.shape[0]

    def key_tile(b, p, jj, j0):
        return jnp.maximum(jj[p], j0[b * n_pairs + p])

    grid_spec = pltpu.PrefetchScalarGridSpec(
        num_scalar_prefetch=3,
        grid=(batch, n_pairs),
        in_specs=[
            pl.BlockSpec((FOX_WIDTH, t), lambda b, p, ii, jj, j0: (0, b * nt + ii[p])),
            pl.BlockSpec((FOX_WIDTH, t),
                         lambda b, p, ii, jj, j0: (1, b * nt + key_tile(b, p, jj, j0))),
            pl.BlockSpec((t, FOX_WIDTH),
                         lambda b, p, ii, jj, j0: (b * nt + key_tile(b, p, jj, j0), 0)),
            pl.BlockSpec((1, t, LANES), lambda b, p, ii, jj, j0: (b, key_tile(b, p, jj, j0), 0)),
        ],
        out_specs=pl.BlockSpec((FOX_WIDTH, t), lambda b, p, ii, jj, j0: (0, b * nt + ii[p])),
        scratch_shapes=[
            pltpu.VMEM((t, t), jnp.float32),
            pltpu.VMEM((t, t), jnp.float32),
            pltpu.VMEM((FOX_HEADS, 1, t), jnp.float32),
            pltpu.VMEM((FOX_HEADS, HEAD_DIM + SUM_ROWS, t), jnp.float32),
        ],
    )
    return pl.pallas_call(
        _fox_kernel,
        grid_spec=grid_spec,
        out_shape=jax.ShapeDtypeStruct((FOX_WIDTH, n), jnp.bfloat16),
        compiler_params=pltpu.CompilerParams(
            dimension_semantics=("arbitrary", "arbitrary"), vmem_limit_bytes=VMEM_LIMIT),
        name="fox_attn",
    )(pairs_i, pairs_j, first_tile, zt, zt, zk, dec)


def _fox_first_tiles(zt, zk, parts, pairs_i, batch, seq, t):
    nt = seq // t
    f32 = jnp.float32
    q = zt[:FOX_WIDTH].astype(f32).reshape(FOX_HEADS, HEAD_DIM, batch, nt, t)
    aq = jnp.sqrt(jnp.max(jnp.sum(q * q, axis=1), axis=-1)).transpose(1, 2, 0)
    k = zk[:, :FOX_WIDTH].astype(f32).reshape(batch, nt, t, FOX_HEADS, HEAD_DIM)
    ak = jnp.sqrt(jnp.max(jnp.sum(k * k, axis=-1), axis=2))
    aq = aq * NORM_SLACK
    ak = ak * NORM_SLACK
    d = sum(p.astype(f32) for p in parts).reshape(batch, FOX_HEADS, nt, t)
    d_first = d[..., 0].transpose(0, 2, 1)
    d_last = d[..., t - 1].transpose(0, 2, 1)
    row_floor = d_first - aq * ak
    tile_top = aq[:, :, None, :] * ak[:, None, :, :] + d_last[:, None, :, :]
    alive = jnp.any(tile_top >= row_floor[:, :, None, :] - DEAD_GAP, axis=-1)
    tile = jnp.arange(nt)
    alive = alive | (tile[None, None, :] >= tile[None, :, None])
    first = jnp.argmax(alive, axis=-1).astype(jnp.int32)
    return first[:, pairs_i].reshape(-1)


def _diff_kernel(i_ref, j_ref, qt_ref, vt_ref, kd_ref, bd_ref, bs_ref, lam_ref, g_ref,
                 o_ref, s0_ref, s1_ref, m_ref, acc_ref, *, lambda_init):
    p_id = pl.program_id(1)
    i = i_ref[p_id]
    j = j_ref[p_id]
    t = qt_ref.shape[1]

    @pl.when(j == 0)
    def _():
        _init_stats(m_ref, acc_ref)

    first_map = lax.broadcasted_iota(jnp.int32, (DIFF_V_DIM, t), 0) < HEAD_DIM
    ones = _ones_rows(t)

    def rows(h):
        return slice(h * DIFF_V_DIM, (h + 1) * DIFF_V_DIM)

    def sweep(kind):
        def scores(u):
            h, mp = divmod(u, 2)
            q = qt_ref[rows(h), :]
            zero = jnp.zeros_like(q)
            qm = jnp.where(first_map, q, zero) if mp == 0 else jnp.where(first_map, zero, q)
            kd = kd_ref[:, rows(h)]
            if kind == "sub":
                s_far = jnp.dot(kd[:t - LANES], qm, preferred_element_type=jnp.float32)
                s_near = jnp.dot(kd[t - LANES:], qm, preferred_element_type=jnp.float32)
                return jnp.concatenate([s_far, s_near + bs_ref[h]], axis=0)
            s = jnp.dot(kd, qm, preferred_element_type=jnp.float32)
            if kind == "diag":
                s = s + bd_ref[h]
            return s

        def values(u):
            return jnp.concatenate([vt_ref[rows(u // 2), :], ones], axis=0)

        _flash_units(2 * DIFF_HEADS, scores, values, (s0_ref, s1_ref), m_ref, acc_ref)

    @pl.when(j < i - 1)
    def _():
        sweep("far")

    @pl.when(j == i - 1)
    def _():
        sweep("sub")

    @pl.when(j == i)
    def _():
        sweep("diag")
        lam_rows = lam_ref[...]
        dot1 = jnp.sum(lam_rows[0:1] * lam_rows[1:2], axis=-1, keepdims=True)
        dot2 = jnp.sum(lam_rows[2:3] * lam_rows[3:4], axis=-1, keepdims=True)
        lam = jnp.exp(dot1) - jnp.exp(dot2) + lambda_init
        g = g_ref[...] * (1.0 - lambda_init)
        for h in range(DIFF_HEADS):
            a1 = acc_ref[2 * h]
            a2 = acc_ref[2 * h + 1]
            d = (a1[:DIFF_V_DIM] / a1[DIFF_V_DIM:DIFF_V_DIM + 1]
                 - lam * (a2[:DIFF_V_DIM] / a2[DIFF_V_DIM:DIFF_V_DIM + 1]))
            y = d * lax.rsqrt(jnp.mean(d * d, axis=0, keepdims=True) + SUBLN_EPS)
            o_ref[rows(h), :] = (y * g).astype(o_ref.dtype)


def _diff_attn(zt, kd, bias_diag, bias_sub, lam_rows, g_col, pairs_i, pairs_j,
               batch, seq, t, lambda_init):
    nt = seq // t
    n = batch * seq
    grid_spec = pltpu.PrefetchScalarGridSpec(
        num_scalar_prefetch=2,
        grid=(batch, pairs_i.shape[0]),
        in_specs=[
            pl.BlockSpec((DIFF_WIDTH, t), lambda b, p, ii, jj: (2, b * nt + ii[p])),
            pl.BlockSpec((DIFF_WIDTH, t), lambda b, p, ii, jj: (3, b * nt + jj[p])),
            pl.BlockSpec((t, DIFF_WIDTH), lambda b, p, ii, jj: (b * nt + jj[p], 1)),
            pl.BlockSpec((DIFF_HEADS, t, t), lambda b, p, ii, jj: (0, 0, 0)),
            pl.BlockSpec((DIFF_HEADS, LANES, t), lambda b, p, ii, jj: (0, 0, 0)),
            pl.BlockSpec((8, LANES), lambda b, p, ii, jj: (0, 0)),
            pl.BlockSpec((DIFF_V_DIM, 1), lambda b, p, ii, jj: (0, 0)),
        ],
        out_specs=pl.BlockSpec((DIFF_WIDTH, t), lambda b, p, ii, jj: (0, b * nt + ii[p])),
        scratch_shapes=[
            pltpu.VMEM((t, t), jnp.float32),
            pltpu.VMEM((t, t), jnp.float32),
            pltpu.VMEM((2 * DIFF_HEADS, 1, t), jnp.float32),
            pltpu.VMEM((2 * DIFF_HEADS, DIFF_V_DIM + SUM_ROWS, t), jnp.float32),
        ],
    )
    return pl.pallas_call(
        functools.partial(_diff_kernel, lambda_init=lambda_init),
        grid_spec=grid_spec,
        out_shape=jax.ShapeDtypeStruct((DIFF_WIDTH, n), jnp.bfloat16),
        compiler_params=pltpu.CompilerParams(
            dimension_semantics=("arbitrary", "arbitrary"), vmem_limit_bytes=VMEM_LIMIT),
        name="diff_attn",
    )(pairs_i, pairs_j, zt, zt, kd, bias_diag, bias_sub, lam_rows, g_col)


def _out_mlp_kernel(x_ref, fox_ref, dif_ref, wo_ref, g_ref, w1_ref, w2_ref, gf_ref, o_ref,
                    *, final_norm):
    y = lax.dot_general(fox_ref[...], wo_ref[:FOX_WIDTH, :], _TN,
                        preferred_element_type=jnp.float32)
    y = y + lax.dot_general(dif_ref[...], wo_ref[FOX_WIDTH:, :], _TN,
                            preferred_element_type=jnp.float32)
    x1 = x_ref[...] + y
    h = _rms(x1, g_ref[...], NORM_EPS).astype(jnp.bfloat16)
    acc = x1
    for c in range(D_FF // FF_CHUNK):
        u = jnp.dot(h, w1_ref[:, c * FF_CHUNK:(c + 1) * FF_CHUNK],
                    preferred_element_type=jnp.float32)
        u = jnp.square(jnp.maximum(u, 0.0)).astype(jnp.bfloat16)
        acc = acc + jnp.dot(u, w2_ref[c * FF_CHUNK:(c + 1) * FF_CHUNK, :],
                            preferred_element_type=jnp.float32)
    if final_norm:
        acc = _rms(acc, gf_ref[...], NORM_EPS)
    o_ref[...] = acc


def _out_mlp(x2d, fox_t, dif_t, wo, g, w1, w2, gf, tm, final_norm):
    n = x2d.shape[0]
    const = lambda i: (0, 0)
    return pl.pallas_call(
        functools.partial(_out_mlp_kernel, final_norm=final_norm),
        grid=(n // tm,),
        in_specs=[
            pl.BlockSpec((tm, D_MODEL), lambda i: (i, 0)),
            pl.BlockSpec((FOX_WIDTH, tm), lambda i: (0, i)),
            pl.BlockSpec((DIFF_WIDTH, tm), lambda i: (0, i)),
            pl.BlockSpec((D_MODEL, D_MODEL), const, pipeline_mode=pl.Buffered(1)),
            pl.BlockSpec((1, D_MODEL), const),
            pl.BlockSpec((D_MODEL, D_FF), const, pipeline_mode=pl.Buffered(1)),
            pl.BlockSpec((D_FF, D_MODEL), const, pipeline_mode=pl.Buffered(1)),
            pl.BlockSpec((1, D_MODEL), const),
        ],
        out_specs=pl.BlockSpec((tm, D_MODEL), lambda i: (i, 0)),
        out_shape=jax.ShapeDtypeStruct((n, D_MODEL), jnp.float32),
        compiler_params=pltpu.CompilerParams(
            dimension_semantics=("arbitrary",), vmem_limit_bytes=VMEM_LIMIT),
        name="out_mlp",
    )(x2d, fox_t, dif_t, wo, g, w1, w2, gf)


def _t5_bucket(rel):
    half = N_BUCKETS // 2
    max_exact = half // 2
    ret = jnp.where(rel > 0, half, 0)
    n = jnp.abs(rel)
    nf = jnp.maximum(n, 1).astype(jnp.float32)
    large = max_exact + (jnp.log(nf / max_exact) / math.log(MAX_DISTANCE / max_exact)
                         * (half - max_exact)).astype(jnp.int32)
    large = jnp.minimum(large, half - 1)
    return ret + jnp.where(n < max_exact, n, large)


def _bias_tiles(table, t):
    tab = table.astype(jnp.float32)

    def lookup(rel):
        bucket = _t5_bucket(rel)
        out = jnp.zeros((tab.shape[1],) + rel.shape, jnp.float32)
        for b in range(N_BUCKETS):
            out = jnp.where(bucket == b, tab[b].reshape((-1,) + (1,) * rel.ndim), out)
        return out

    far = lookup(jnp.full((1, 1), -MAX_DISTANCE, jnp.int32))
    k = jnp.arange(t)[:, None]
    q = jnp.arange(t)[None, :]
    rel = k - q
    diag = (lookup(rel) - far) * LOG2E
    diag = jnp.where((k // CHUNK) <= (q // CHUNK), diag, NEG_BIG)
    rel_sub = rel[t - LANES:] - t
    sub = (lookup(rel_sub) - far) * LOG2E
    return diag, sub


def _split_bf16(x, parts):
    out = []
    for _ in range(parts - 1):
        bits = lax.bitcast_convert_type(x, jnp.uint32) & jnp.uint32(0xFFFF0000)
        piece = lax.bitcast_convert_type(bits, jnp.float32)
        out.append(piece.astype(jnp.bfloat16))
        x = x - piece
    out.append(x.astype(jnp.bfloat16))
    return out


def _forward(x, w_in, b_f, lambda_q1, lambda_k1, lambda_q2, lambda_k2, subln_g, w_out,
             norm_attn_g, norm_mlp_g, w_mlp_in, w_mlp_out, rel_bias_table, final_norm_g,
             *, attn_tile, row_tile):
    batch, seq, _ = x.shape
    depth = w_in.shape[0]
    n = batch * seq
    t = min(attn_tile, seq)
    tm = min(row_tile, n)
    assert seq % t == 0 and n % tm == 0 and seq % LANES == 0 and t % LANES == 0 and t > LANES
    nt = seq // t
    pairs = [(i, j) for i in range(nt) for j in range(i + 1)]
    pairs_i = jnp.asarray([p[0] for p in pairs], jnp.int32)
    pairs_j = jnp.asarray([p[1] for p in pairs], jnp.int32)
    fox_pairs = [(i, j) for i in range(nt) for j in range(i, -1, -1)]
    fox_i = jnp.asarray([p[0] for p in fox_pairs], jnp.int32)
    fox_j = jnp.asarray([p[1] for p in fox_pairs], jnp.int32)
    bias_diag, bias_sub = _bias_tiles(rel_bias_table, t)

    o_fq, o_fk, o_fv = 0, FOX_WIDTH, 2 * FOX_WIDTH
    o_ff = 3 * FOX_WIDTH
    o_dq = o_ff + FOX_HEADS
    o_dk = o_dq + DIFF_WIDTH
    o_dv = o_dk + DIFF_WIDTH
    qscale = LOG2E / math.sqrt(HEAD_DIM)

    x2d = x.reshape(n, D_MODEL)
    for l in range(depth):
        w = w_in[l]
        wqv_t = jnp.concatenate([
            w[:, o_fq:o_fq + FOX_WIDTH] * qscale, w[:, o_fv:o_fv + FOX_WIDTH],
            w[:, o_dq:o_dq + DIFF_WIDTH] * qscale, w[:, o_dv:o_dv + DIFF_WIDTH]],
            axis=1).T.astype(jnp.bfloat16)
        wk = jnp.concatenate([w[:, o_fk:o_fk + FOX_WIDTH], w[:, o_dk:o_dk + DIFF_WIDTH]],
                             axis=1).astype(jnp.bfloat16)
        wf = jnp.pad(w[:, o_ff:o_ff + FOX_HEADS], ((0, 0), (0, LANES - FOX_HEADS))
                     ).astype(jnp.bfloat16)
        bf = jnp.pad(b_f[l].astype(jnp.float32), (0, LANES - FOX_HEADS)).reshape(1, LANES)

        zt, zk, lf = _in_proj(x2d, norm_attn_g[l].reshape(1, D_MODEL), wqv_t, wk, wf, bf, tm)

        lf_rows = lf[:, :FOX_HEADS].reshape(batch, seq, FOX_HEADS).transpose(0, 2, 1)
        c = _decay_scan(lf_rows.reshape(batch * FOX_HEADS * seq // LANES, LANES), seq // LANES)
        c = c.reshape(batch, FOX_HEADS, seq)
        parts = _split_bf16(c * (-LOG2E), N_DECAY_PARTS)
        dec = jnp.stack(parts, axis=2).transpose(0, 3, 1, 2)
        dec = jnp.pad(dec.reshape(batch, seq, FOX_HEADS * N_DECAY_PARTS),
                      ((0, 0), (0, 0), (0, LANES - FOX_HEADS * N_DECAY_PARTS)))

        first_tile = _fox_first_tiles(zt, zk, parts, fox_i, batch, seq, t)
        fox_t = _fox_attn(zt, zk, dec, fox_i, fox_j, first_tile, batch, seq, t)

        lambda_init = 0.8 - 0.6 * math.exp(-0.3 * l)
        lam_rows = jnp.pad(
            jnp.stack([lambda_q1[l], lambda_k1[l], lambda_q2[l], lambda_k2[l]]).astype(jnp.float32),
            ((0, 4), (0, LANES - HEAD_DIM)))
        dif_t = _diff_attn(zt, zk, bias_diag, bias_sub, lam_rows,
                           subln_g[l].astype(jnp.float32).reshape(DIFF_V_DIM, 1),
                           pairs_i, pairs_j, batch, seq, t, lambda_init)

        x2d = _out_mlp(x2d, fox_t, dif_t, w_out[l].astype(jnp.bfloat16),
                       norm_mlp_g[l].reshape(1, D_MODEL), w_mlp_in[l].astype(jnp.bfloat16),
                       w_mlp_out[l].astype(jnp.bfloat16), final_norm_g.reshape(1, D_MODEL),
                       tm, final_norm=(l == depth - 1))
    return x2d.reshape(batch, seq, D_MODEL)


def kernel(x, w_in, b_f, lambda_q1, lambda_k1, lambda_q2, lambda_k2, subln_g, w_out,
           norm_attn_g, norm_mlp_g, w_mlp_in, w_mlp_out, rel_bias_table, final_norm_g):
    return _forward(x, w_in, b_f, lambda_q1, lambda_k1, lambda_q2, lambda_k2, subln_g, w_out,
                    norm_attn_g, norm_mlp_g, w_mlp_in, w_mlp_out, rel_bias_table, final_norm_g,
                    attn_tile=ATTN_TILE, row_tile=ROW_TILE)
```

```python
import functools
import math

import jax
import jax.numpy as jnp
import numpy as np
from jax import lax
from jax.experimental import pallas as pl
from jax.experimental.pallas import tpu as pltpu

D_MODEL = 1024
CHUNK = 64
FOX_HEADS = 8
HEAD_DIM = 64
FOX_WIDTH = FOX_HEADS * HEAD_DIM
DIFF_HEADS = 4
DIFF_V_DIM = 2 * HEAD_DIM
DIFF_WIDTH = DIFF_HEADS * DIFF_V_DIM
D_FF = 4 * D_MODEL
N_BUCKETS = 32
MAX_DISTANCE = 128
NORM_EPS = 1e-5
SUBLN_EPS = 1e-5

LANES = 128
LOG2E = math.log2(math.e)
NEG_BIG = -1e30
N_DECAY_PARTS = 3
SUM_ROWS = 16
Q_PARTS = 2
DEAD_GAP = 152.0
NORM_SLACK = 1.001
ATTN_TILE = 512
ROW_TILE = 512
FF_CHUNK = 1024
VMEM_LIMIT = 48 * 1024 * 1024

_NT = (((1,), (1,)), ((), ()))
_TN = (((0,), (0,)), ((), ()))


def _rms(x, g, eps):
    return x * lax.rsqrt(jnp.mean(x * x, axis=-1, keepdims=True) + eps) * g


def _in_proj_kernel(x_ref, g_ref, wqv_ref, wk_ref, wf_ref, bf_ref, zt_ref, zk_ref, lf_ref):
    h = _rms(x_ref[...], g_ref[...], NORM_EPS).astype(jnp.bfloat16)
    zt = lax.dot_general(wqv_ref[...], h, _NT, preferred_element_type=jnp.float32)
    zt_ref[...] = zt.astype(jnp.bfloat16)
    zk_ref[...] = jnp.dot(h, wk_ref[...], preferred_element_type=jnp.float32).astype(jnp.bfloat16)
    f = jnp.dot(h, wf_ref[...], preferred_element_type=jnp.float32) + bf_ref[...]
    lf_ref[...] = jnp.minimum(f, 0.0) - jnp.log1p(jnp.exp(-jnp.abs(f)))


def _in_proj(x2d, g, wqv_t, wk, wf, bf, tm):
    n = x2d.shape[0]
    nqv = wqv_t.shape[0]
    nk = wk.shape[1]
    const = lambda i: (0, 0)
    return pl.pallas_call(
        _in_proj_kernel,
        grid=(n // tm,),
        in_specs=[
            pl.BlockSpec((tm, D_MODEL), lambda i: (i, 0)),
            pl.BlockSpec((1, D_MODEL), const),
            pl.BlockSpec((nqv, D_MODEL), const),
            pl.BlockSpec((D_MODEL, nk), const),
            pl.BlockSpec((D_MODEL, LANES), const),
            pl.BlockSpec((1, LANES), const),
        ],
        out_specs=[
            pl.BlockSpec((nqv, tm), lambda i: (0, i)),
            pl.BlockSpec((tm, nk), lambda i: (i, 0)),
            pl.BlockSpec((tm, LANES), lambda i: (i, 0)),
        ],
        out_shape=[
            jax.ShapeDtypeStruct((nqv, n), jnp.bfloat16),
            jax.ShapeDtypeStruct((n, nk), jnp.bfloat16),
            jax.ShapeDtypeStruct((n, LANES), jnp.float32),
        ],
        compiler_params=pltpu.CompilerParams(
            dimension_semantics=("arbitrary",), vmem_limit_bytes=VMEM_LIMIT),
        name="in_proj",
    )(x2d, g, wqv_t, wk, wf, bf)


def _decay_scan_kernel(x_ref, o_ref, *, rows_per_seq):
    x = x_ref[...]
    lane = lax.broadcasted_iota(jnp.int32, x.shape, 1)
    d = 1
    while d < LANES:
        x = x + jnp.where(lane >= d, pltpu.roll(x, d, 1), 0.0)
        d *= 2
    tot = jnp.broadcast_to(x[:, LANES - 1:LANES], x.shape)
    row = lax.broadcasted_iota(jnp.int32, x.shape, 0) % rows_per_seq
    inc = tot
    d = 1
    while d < rows_per_seq:
        inc = inc + jnp.where(row >= d, pltpu.roll(inc, d, 0), 0.0)
        d *= 2
    o_ref[...] = x + (inc - tot)


def _decay_scan(lf_rows, rows_per_seq):
    return pl.pallas_call(
        functools.partial(_decay_scan_kernel, rows_per_seq=rows_per_seq),
        out_shape=jax.ShapeDtypeStruct(lf_rows.shape, jnp.float32),
        compiler_params=pltpu.CompilerParams(vmem_limit_bytes=VMEM_LIMIT),
        name="decay_scan",
    )(lf_rows)


def _ones_rows(t):
    return jnp.where(lax.broadcasted_iota(jnp.int32, (SUM_ROWS, t), 0) == 0, 1.0, 0.0
                     ).astype(jnp.bfloat16)


def _flash_units(n_groups, t, kinds, score_fn, value_fn, s_bufs, m_ref, acc_ref):
    units = [(g, slice(c * t, (c + 1) * t), kind) for g in range(n_groups)
             for c, kind in enumerate(kinds) if kind is not None]

    def produce(n):
        g, cols, kind = units[n]
        s = score_fn(g, cols, kind)
        s_bufs[n % 2][...] = s
        return jnp.max(s, axis=0, keepdims=True)

    mx = produce(0)
    for n, (g, cols, _) in enumerate(units):
        mx_next = produce(n + 1) if n + 1 < len(units) else None
        m_prev = m_ref[g, :, cols]
        m_new = jnp.maximum(m_prev, mx)
        m_ref[g, :, cols] = m_new
        p = jnp.exp2(s_bufs[n % 2][...] - m_new).astype(jnp.bfloat16)
        pv = jnp.dot(value_fn(g), p, preferred_element_type=jnp.float32)
        acc_ref[g, :, cols] = jnp.exp2(m_prev - m_new) * acc_ref[g, :, cols] + pv
        mx = mx_next


def _init_stats(m_ref, acc_ref):
    m_ref[...] = jnp.full(m_ref.shape, NEG_BIG, jnp.float32)
    acc_ref[...] = jnp.zeros(acc_ref.shape, jnp.float32)


def _fox_kernel(i_ref, j_ref, j0_ref, qt_ref, vt_ref, k_ref, dec_ref, o_ref,
                s0_ref, s1_ref, m_ref, acc_ref):
    p_id = pl.program_id(1)
    t = s0_ref.shape[0]
    parts = qt_ref.shape[1] // t
    rel = j_ref[p_id] - parts * i_ref[p_id]
    j = j_ref[p_id]
    j0 = j0_ref[pl.program_id(0) * pl.num_programs(1) + p_id]

    @pl.when(rel == parts - 1)
    def _():
        _init_stats(m_ref, acc_ref)

    ones = _ones_rows(t)
    zeros = jnp.zeros((HEAD_DIM, t), jnp.bfloat16)
    dec_row = lax.broadcasted_iota(jnp.int32, (LANES, t), 0)

    def rows(h):
        return slice(h * HEAD_DIM, (h + 1) * HEAD_DIM)

    def scores(h, cols, kind):
        q = qt_ref[rows(h), cols]
        pick = ((dec_row >= N_DECAY_PARTS * h) & (dec_row < N_DECAY_PARTS * (h + 1)))
        qa = jnp.concatenate(([q, zeros] if h % 2 == 0 else [zeros, q])
                             + [jnp.where(pick, 1.0, 0.0).astype(jnp.bfloat16)], axis=0)
        pair = slice((h // 2) * LANES, (h // 2 + 1) * LANES)
        ka = jnp.concatenate([k_ref[:, pair], dec_ref[0]], axis=1)
        s = jnp.dot(ka, qa, preferred_element_type=jnp.float32)
        if kind == "diag":
            kpos = lax.broadcasted_iota(jnp.int32, s.shape, 0)
            qpos = lax.broadcasted_iota(jnp.int32, s.shape, 1)
            s = jnp.where(kpos <= qpos, s, NEG_BIG)
        return s

    def values(h):
        return jnp.concatenate([vt_ref[rows(h), :], ones], axis=0)

    def sweep(kinds):
        _flash_units(FOX_HEADS, t, kinds, scores, values, (s0_ref, s1_ref), m_ref, acc_ref)

    for r in range(parts):
        @pl.when(rel == r)
        def _(r=r):
            sweep([None] * r + ["diag"] + ["past"] * (parts - 1 - r))

    @pl.when((rel < 0) & (j >= j0))
    def _():
        sweep(["past"] * parts)

    @pl.when(j == 0)
    def _():
        for h in range(FOX_HEADS):
            a = acc_ref[h]
            o_ref[rows(h), :] = (a[:HEAD_DIM] / a[HEAD_DIM:HEAD_DIM + 1]).astype(o_ref.dtype)


def _fox_attn(zt, zk, dec, pairs_i, pairs_j, first_tile, batch, seq, t, parts):
    nt = seq // t
    tq = parts * t
    nq = seq // tq
    n = batch * seq
    n_pairs = pairs_i.shape[0]

    def key_tile(b, p, jj, j0):
        return jnp.maximum(jj[p], j0[b * n_pairs + p])

    grid_spec = pltpu.PrefetchScalarGridSpec(
        num_scalar_prefetch=3,
        grid=(batch, n_pairs),
        in_specs=[
            pl.BlockSpec((FOX_WIDTH, tq), lambda b, p, ii, jj, j0: (0, b * nq + ii[p])),
            pl.BlockSpec((FOX_WIDTH, t),
                         lambda b, p, ii, jj, j0: (1, b * nt + key_tile(b, p, jj, j0))),
            pl.BlockSpec((t, FOX_WIDTH),
                         lambda b, p, ii, jj, j0: (b * nt + key_tile(b, p, jj, j0), 0)),
            pl.BlockSpec((1, t, LANES), lambda b, p, ii, jj, j0: (b, key_tile(b, p, jj, j0), 0)),
        ],
        out_specs=pl.BlockSpec((FOX_WIDTH, tq), lambda b, p, ii, jj, j0: (0, b * nq + ii[p])),
        scratch_shapes=[
            pltpu.VMEM((t, t), jnp.float32),
            pltpu.VMEM((t, t), jnp.float32),
            pltpu.VMEM((FOX_HEADS, 1, tq), jnp.float32),
            pltpu.VMEM((FOX_HEADS, HEAD_DIM + SUM_ROWS, tq), jnp.float32),
        ],
    )
    return pl.pallas_call(
        _fox_kernel,
        grid_spec=grid_spec,
        out_shape=jax.ShapeDtypeStruct((FOX_WIDTH, n), jnp.bfloat16),
        compiler_params=pltpu.CompilerParams(
            dimension_semantics=("arbitrary", "arbitrary"), vmem_limit_bytes=VMEM_LIMIT),
        name="fox_attn",
    )(pairs_i, pairs_j, first_tile, zt, zt, zk, dec)


def _fox_first_tiles(zt, zk, pieces, pairs_i, batch, seq, t, parts):
    nt = seq // t
    nq = nt // parts
    f32 = jnp.float32
    q = zt[:FOX_WIDTH].astype(f32).reshape(FOX_HEADS, HEAD_DIM, batch, nq, parts * t)
    aq = jnp.sqrt(jnp.max(jnp.sum(q * q, axis=1), axis=-1)).transpose(1, 2, 0)
    k = zk[:, :FOX_WIDTH].astype(f32).reshape(batch, nt, t, FOX_HEADS, HEAD_DIM)
    ak = jnp.sqrt(jnp.max(jnp.sum(k * k, axis=-1), axis=2))
    aq = aq * NORM_SLACK
    ak = ak * NORM_SLACK
    d = sum(p.astype(f32) for p in pieces).reshape(batch, FOX_HEADS, nt, t)
    d_first = d[:, :, ::parts, 0].transpose(0, 2, 1)
    d_last = d[..., t - 1].transpose(0, 2, 1)
    own_k = jnp.max(ak.reshape(batch, nq, parts, FOX_HEADS), axis=2)
    row_floor = d_first - aq * own_k
    tile_top = aq[:, :, None, :] * ak[:, None, :, :] + d_last[:, None, :, :]
    alive = jnp.any(tile_top >= row_floor[:, :, None, :] - DEAD_GAP, axis=-1)
    alive = alive | (jnp.arange(nt)[None, None, :] >= parts * jnp.arange(nq)[None, :, None])
    first = jnp.argmax(alive, axis=-1).astype(jnp.int32)
    return first[:, pairs_i].reshape(-1)


def _diff_kernel(i_ref, j_ref, qt_ref, vt_ref, kd_ref, bd_ref, bs_ref, lam_ref, g_ref,
                 o_ref, s0_ref, s1_ref, m_ref, acc_ref, *, lambda_init):
    p_id = pl.program_id(1)
    t = s0_ref.shape[0]
    parts = qt_ref.shape[1] // t
    j = j_ref[p_id]
    rel = j - parts * i_ref[p_id]

    @pl.when(j == 0)
    def _():
        _init_stats(m_ref, acc_ref)

    first_map = lax.broadcasted_iota(jnp.int32, (DIFF_V_DIM, t), 0) < HEAD_DIM
    ones = _ones_rows(t)

    def rows(h):
        return slice(h * DIFF_V_DIM, (h + 1) * DIFF_V_DIM)

    def scores(u, cols, kind):
        h, mp = divmod(u, 2)
        q = qt_ref[rows(h), cols]
        zero = jnp.zeros_like(q)
        qm = jnp.where(first_map, q, zero) if mp == 0 else jnp.where(first_map, zero, q)
        kd = kd_ref[:, rows(h)]
        if kind == "sub":
            s_far = jnp.dot(kd[:t - LANES], qm, preferred_element_type=jnp.float32)
            s_near = jnp.dot(kd[t - LANES:], qm, preferred_element_type=jnp.float32)
            return jnp.concatenate([s_far, s_near + bs_ref[h]], axis=0)
        s = jnp.dot(kd, qm, preferred_element_type=jnp.float32)
        if kind == "diag":
            s = s + bd_ref[h]
        return s

    def values(u):
        return jnp.concatenate([vt_ref[rows(u // 2), :], ones], axis=0)

    def kind_of(d):
        return None if d > 0 else "diag" if d == 0 else "sub" if d == -1 else "far"

    def sweep(r):
        kinds = [kind_of(r - c) for c in range(parts)]
        _flash_units(2 * DIFF_HEADS, t, kinds, scores, values, (s0_ref, s1_ref), m_ref, acc_ref)

    @pl.when(rel < -1)
    def _():
        sweep(-2)

    for r in range(-1, parts):
        @pl.when(rel == r)
        def _(r=r):
            sweep(r)

    @pl.when(rel == parts - 1)
    def _():
        lam_rows = lam_ref[...]
        dot1 = jnp.sum(lam_rows[0:1] * lam_rows[1:2], axis=-1, keepdims=True)
        dot2 = jnp.sum(lam_rows[2:3] * lam_rows[3:4], axis=-1, keepdims=True)
        lam = jnp.exp(dot1) - jnp.exp(dot2) + lambda_init
        g = g_ref[...] * (1.0 - lambda_init)
        for h in range(DIFF_HEADS):
            a1 = acc_ref[2 * h]
            a2 = acc_ref[2 * h + 1]
            d = (a1[:DIFF_V_DIM] / a1[DIFF_V_DIM:DIFF_V_DIM + 1]
                 - lam * (a2[:DIFF_V_DIM] / a2[DIFF_V_DIM:DIFF_V_DIM + 1]))
            y = d * lax.rsqrt(jnp.mean(d * d, axis=0, keepdims=True) + SUBLN_EPS)
            o_ref[rows(h), :] = (y * g).astype(o_ref.dtype)


def _diff_attn(zt, kd, bias_diag, bias_sub, lam_rows, g_col, pairs_i, pairs_j,
               batch, seq, t, parts, lambda_init):
    nt = seq // t
    tq = parts * t
    nq = seq // tq
    n = batch * seq
    grid_spec = pltpu.PrefetchScalarGridSpec(
        num_scalar_prefetch=2,
        grid=(batch, pairs_i.shape[0]),
        in_specs=[
            pl.BlockSpec((DIFF_WIDTH, tq), lambda b, p, ii, jj: (2, b * nq + ii[p])),
            pl.BlockSpec((DIFF_WIDTH, t), lambda b, p, ii, jj: (3, b * nt + jj[p])),
            pl.BlockSpec((t, DIFF_WIDTH), lambda b, p, ii, jj: (b * nt + jj[p], 1)),
            pl.BlockSpec((DIFF_HEADS, t, t), lambda b, p, ii, jj: (0, 0, 0)),
            pl.BlockSpec((DIFF_HEADS, LANES, t), lambda b, p, ii, jj: (0, 0, 0)),
            pl.BlockSpec((8, LANES), lambda b, p, ii, jj: (0, 0)),
            pl.BlockSpec((DIFF_V_DIM, 1), lambda b, p, ii, jj: (0, 0)),
        ],
        out_specs=pl.BlockSpec((DIFF_WIDTH, tq), lambda b, p, ii, jj: (0, b * nq + ii[p])),
        scratch_shapes=[
            pltpu.VMEM((t, t), jnp.float32),
            pltpu.VMEM((t, t), jnp.float32),
            pltpu.VMEM((2 * DIFF_HEADS, 1, tq), jnp.float32),
            pltpu.VMEM((2 * DIFF_HEADS, DIFF_V_DIM + SUM_ROWS, tq), jnp.float32),
        ],
    )
    return pl.pallas_call(
        functools.partial(_diff_kernel, lambda_init=lambda_init),
        grid_spec=grid_spec,
        out_shape=jax.ShapeDtypeStruct((DIFF_WIDTH, n), jnp.bfloat16),
        compiler_params=pltpu.CompilerParams(
            dimension_semantics=("arbitrary", "arbitrary"), vmem_limit_bytes=VMEM_LIMIT),
        name="diff_attn",
    )(pairs_i, pairs_j, zt, zt, kd, bias_diag, bias_sub, lam_rows, g_col)


def _out_mlp_kernel(x_ref, fox_ref, dif_ref, wo_ref, g_ref, w1_ref, w2_ref, gf_ref, o_ref,
                    *, final_norm):
    y = lax.dot_general(fox_ref[...], wo_ref[:FOX_WIDTH, :], _TN,
                        preferred_element_type=jnp.float32)
    y = y + lax.dot_general(dif_ref[...], wo_ref[FOX_WIDTH:, :], _TN,
                            preferred_element_type=jnp.float32)
    x1 = x_ref[...] + y
    h = _rms(x1, g_ref[...], NORM_EPS).astype(jnp.bfloat16)
    acc = x1
    for c in range(D_FF // FF_CHUNK):
        u = jnp.dot(h, w1_ref[:, c * FF_CHUNK:(c + 1) * FF_CHUNK],
                    preferred_element_type=jnp.float32)
        u = jnp.square(jnp.maximum(u, 0.0)).astype(jnp.bfloat16)
        acc = acc + jnp.dot(u, w2_ref[c * FF_CHUNK:(c + 1) * FF_CHUNK, :],
                            preferred_element_type=jnp.float32)
    if final_norm:
        acc = _rms(acc, gf_ref[...], NORM_EPS)
    o_ref[...] = acc


def _out_mlp(x2d, fox_t, dif_t, wo, g, w1, w2, gf, tm, final_norm):
    n = x2d.shape[0]
    const = lambda i: (0, 0)
    return pl.pallas_call(
        functools.partial(_out_mlp_kernel, final_norm=final_norm),
        grid=(n // tm,),
        in_specs=[
            pl.BlockSpec((tm, D_MODEL), lambda i: (i, 0)),
            pl.BlockSpec((FOX_WIDTH, tm), lambda i: (0, i)),
            pl.BlockSpec((DIFF_WIDTH, tm), lambda i: (0, i)),
            pl.BlockSpec((D_MODEL, D_MODEL), const, pipeline_mode=pl.Buffered(1)),
            pl.BlockSpec((1, D_MODEL), const),
            pl.BlockSpec((D_MODEL, D_FF), const, pipeline_mode=pl.Buffered(1)),
            pl.BlockSpec((D_FF, D_MODEL), const, pipeline_mode=pl.Buffered(1)),
            pl.BlockSpec((1, D_MODEL), const),
        ],
        out_specs=pl.BlockSpec((tm, D_MODEL), lambda i: (i, 0)),
        out_shape=jax.ShapeDtypeStruct((n, D_MODEL), jnp.float32),
        compiler_params=pltpu.CompilerParams(
            dimension_semantics=("arbitrary",), vmem_limit_bytes=VMEM_LIMIT),
        name="out_mlp",
    )(x2d, fox_t, dif_t, wo, g, w1, w2, gf)


def _t5_bucket(rel):
    half = N_BUCKETS // 2
    max_exact = half // 2
    ret = jnp.where(rel > 0, half, 0)
    n = jnp.abs(rel)
    nf = jnp.maximum(n, 1).astype(jnp.float32)
    large = max_exact + (jnp.log(nf / max_exact) / math.log(MAX_DISTANCE / max_exact)
                         * (half - max_exact)).astype(jnp.int32)
    large = jnp.minimum(large, half - 1)
    return ret + jnp.where(n < max_exact, n, large)


def _bias_tiles(table, t):
    tab = table.astype(jnp.float32)

    def lookup(rel):
        bucket = _t5_bucket(rel)
        out = jnp.zeros((tab.shape[1],) + rel.shape, jnp.float32)
        for b in range(N_BUCKETS):
            out = jnp.where(bucket == b, tab[b].reshape((-1,) + (1,) * rel.ndim), out)
        return out

    far = lookup(jnp.full((1, 1), -MAX_DISTANCE, jnp.int32))
    k = jnp.arange(t)[:, None]
    q = jnp.arange(t)[None, :]
    rel = k - q
    diag = (lookup(rel) - far) * LOG2E
    diag = jnp.where((k // CHUNK) <= (q // CHUNK), diag, NEG_BIG)
    rel_sub = rel[t - LANES:] - t
    sub = (lookup(rel_sub) - far) * LOG2E
    return diag, sub


def _split_bf16(x, parts):
    out = []
    for _ in range(parts - 1):
        bits = lax.bitcast_convert_type(x, jnp.uint32) & jnp.uint32(0xFFFF0000)
        piece = lax.bitcast_convert_type(bits, jnp.float32)
        out.append(piece.astype(jnp.bfloat16))
        x = x - piece
    out.append(x.astype(jnp.bfloat16))
    return out


def _forward(x, w_in, b_f, lambda_q1, lambda_k1, lambda_q2, lambda_k2, subln_g, w_out,
             norm_attn_g, norm_mlp_g, w_mlp_in, w_mlp_out, rel_bias_table, final_norm_g,
             *, attn_tile, row_tile):
    batch, seq, _ = x.shape
    depth = w_in.shape[0]
    n = batch * seq
    t = attn_tile
    tm = min(row_tile, n)
    qp = Q_PARTS
    assert seq % (qp * t) == 0 and n % tm == 0 and t % LANES == 0 and t > LANES
    nq = seq // (qp * t)
    pairs = [(i, j) for i in range(nq) for j in range(qp * (i + 1))]
    pairs_i = jnp.asarray([p[0] for p in pairs], jnp.int32)
    pairs_j = jnp.asarray([p[1] for p in pairs], jnp.int32)
    fox_pairs = [(i, j) for i in range(nq) for j in range(qp * (i + 1) - 1, -1, -1)]
    fox_i = jnp.asarray([p[0] for p in fox_pairs], jnp.int32)
    fox_j = jnp.asarray([p[1] for p in fox_pairs], jnp.int32)
    bias_diag, bias_sub = _bias_tiles(rel_bias_table, t)

    o_fq, o_fk, o_fv = 0, FOX_WIDTH, 2 * FOX_WIDTH
    o_ff = 3 * FOX_WIDTH
    o_dq = o_ff + FOX_HEADS
    o_dk = o_dq + DIFF_WIDTH
    o_dv = o_dk + DIFF_WIDTH
    qscale = LOG2E / math.sqrt(HEAD_DIM)

    x2d = x.reshape(n, D_MODEL)
    for l in range(depth):
        w = w_in[l]
        wqv_t = jnp.concatenate([
            w[:, o_fq:o_fq + FOX_WIDTH] * qscale, w[:, o_fv:o_fv + FOX_WIDTH],
            w[:, o_dq:o_dq + DIFF_WIDTH] * qscale, w[:, o_dv:o_dv + DIFF_WIDTH]],
            axis=1).T.astype(jnp.bfloat16)
        wk = jnp.concatenate([w[:, o_fk:o_fk + FOX_WIDTH], w[:, o_dk:o_dk + DIFF_WIDTH]],
                             axis=1).astype(jnp.bfloat16)
        wf = jnp.pad(w[:, o_ff:o_ff + FOX_HEADS], ((0, 0), (0, LANES - FOX_HEADS))
                     ).astype(jnp.bfloat16)
        bf = jnp.pad(b_f[l].astype(jnp.float32), (0, LANES - FOX_HEADS)).reshape(1, LANES)

        zt, zk, lf = _in_proj(x2d, norm_attn_g[l].reshape(1, D_MODEL), wqv_t, wk, wf, bf, tm)

        lf_rows = lf[:, :FOX_HEADS].reshape(batch, seq, FOX_HEADS).transpose(0, 2, 1)
        c = _decay_scan(lf_rows.reshape(batch * FOX_HEADS * seq // LANES, LANES), seq // LANES)
        c = c.reshape(batch, FOX_HEADS, seq)
        parts = _split_bf16(c * (-LOG2E), N_DECAY_PARTS)
        dec = jnp.stack(parts, axis=2).transpose(0, 3, 1, 2)
        dec = jnp.pad(dec.reshape(batch, seq, FOX_HEADS * N_DECAY_PARTS),
                      ((0, 0), (0, 0), (0, LANES - FOX_HEADS * N_DECAY_PARTS)))

        first_tile = _fox_first_tiles(zt, zk, parts, fox_i, batch, seq, t, qp)
        fox_t = _fox_attn(zt, zk, dec, fox_i, fox_j, first_tile, batch, seq, t, qp)

        lambda_init = 0.8 - 0.6 * math.exp(-0.3 * l)
        lam_rows = jnp.pad(
            jnp.stack([lambda_q1[l], lambda_k1[l], lambda_q2[l], lambda_k2[l]]).astype(jnp.float32),
            ((0, 4), (0, LANES - HEAD_DIM)))
        dif_t = _diff_attn(zt, zk, bias_diag, bias_sub, lam_rows,
                           subln_g[l].astype(jnp.float32).reshape(DIFF_V_DIM, 1),
                           pairs_i, pairs_j, batch, seq, t, qp, lambda_init)

        x2d = _out_mlp(x2d, fox_t, dif_t, w_out[l].astype(jnp.bfloat16),
                       norm_mlp_g[l].reshape(1, D_MODEL), w_mlp_in[l].astype(jnp.bfloat16),
                       w_mlp_out[l].astype(jnp.bfloat16), final_norm_g.reshape(1, D_MODEL),
                       tm, final_norm=(l == depth - 1))
    return x2d.reshape(batch, seq, D_MODEL)


def kernel(x, w_in, b_f, lambda_q1, lambda_k1, lambda_q2, lambda_k2, subln_g, w_out,
           norm_attn_g, norm_mlp_g, w_mlp_in, w_mlp_out, rel_bias_table, final_norm_g):
    return _forward(x, w_in, b_f, lambda_q1, lambda_k1, lambda_q2, lambda_k2, subln_g, w_out,
                    norm_attn_g, norm_mlp_g, w_mlp_in, w_mlp_out, rel_bias_table, final_norm_g,
                    attn_tile=ATTN_TILE, row_tile=ROW_TILE)
```

```python
import functools
import math

import jax
import jax.numpy as jnp
import numpy as np
from jax import lax
from jax.experimental import pallas as pl
from jax.experimental.pallas import tpu as pltpu

D_MODEL = 1024
CHUNK = 64
FOX_HEADS = 8
HEAD_DIM = 64
FOX_WIDTH = FOX_HEADS * HEAD_DIM
DIFF_HEADS = 4
DIFF_V_DIM = 2 * HEAD_DIM
DIFF_WIDTH = DIFF_HEADS * DIFF_V_DIM
D_FF = 4 * D_MODEL
N_BUCKETS = 32
MAX_DISTANCE = 128
NORM_EPS = 1e-5
SUBLN_EPS = 1e-5

LANES = 128
LOG2E = math.log2(math.e)
NEG_BIG = -1e30
N_DECAY_PARTS = 3
SUM_ROWS = 16
FOX_Q_PARTS = 2
DIFF_Q_PARTS = 4
DEAD_GAP = 152.0
NORM_SLACK = 1.001
ATTN_TILE = 512
ROW_TILE = 512
FF_CHUNK = 1024
VMEM_LIMIT = 48 * 1024 * 1024

_NT = (((1,), (1,)), ((), ()))
_TN = (((0,), (0,)), ((), ()))


def _rms(x, g, eps):
    return x * lax.rsqrt(jnp.mean(x * x, axis=-1, keepdims=True) + eps) * g


def _in_proj_kernel(x_ref, g_ref, wqv_ref, wk_ref, wf_ref, bf_ref, zt_ref, zk_ref, lf_ref):
    h = _rms(x_ref[...], g_ref[...], NORM_EPS).astype(jnp.bfloat16)
    zt = lax.dot_general(wqv_ref[...], h, _NT, preferred_element_type=jnp.float32)
    zt_ref[...] = zt.astype(jnp.bfloat16)
    zk_ref[...] = jnp.dot(h, wk_ref[...], preferred_element_type=jnp.float32).astype(jnp.bfloat16)
    f = jnp.dot(h, wf_ref[...], preferred_element_type=jnp.float32) + bf_ref[...]
    lf_ref[...] = jnp.minimum(f, 0.0) - jnp.log1p(jnp.exp(-jnp.abs(f)))


def _in_proj(x2d, g, wqv_t, wk, wf, bf, tm):
    n = x2d.shape[0]
    nqv = wqv_t.shape[0]
    nk = wk.shape[1]
    const = lambda i: (0, 0)
    return pl.pallas_call(
        _in_proj_kernel,
        grid=(n // tm,),
        in_specs=[
            pl.BlockSpec((tm, D_MODEL), lambda i: (i, 0)),
            pl.BlockSpec((1, D_MODEL), const),
            pl.BlockSpec((nqv, D_MODEL), const),
            pl.BlockSpec((D_MODEL, nk), const),
            pl.BlockSpec((D_MODEL, LANES), const),
            pl.BlockSpec((1, LANES), const),
        ],
        out_specs=[
            pl.BlockSpec((nqv, tm), lambda i: (0, i)),
            pl.BlockSpec((tm, nk), lambda i: (i, 0)),
            pl.BlockSpec((tm, LANES), lambda i: (i, 0)),
        ],
        out_shape=[
            jax.ShapeDtypeStruct((nqv, n), jnp.bfloat16),
            jax.ShapeDtypeStruct((n, nk), jnp.bfloat16),
            jax.ShapeDtypeStruct((n, LANES), jnp.float32),
        ],
        compiler_params=pltpu.CompilerParams(
            dimension_semantics=("arbitrary",), vmem_limit_bytes=VMEM_LIMIT),
        name="in_proj",
    )(x2d, g, wqv_t, wk, wf, bf)


def _decay_scan_kernel(x_ref, o_ref, *, rows_per_seq):
    x = x_ref[...]
    lane = lax.broadcasted_iota(jnp.int32, x.shape, 1)
    d = 1
    while d < LANES:
        x = x + jnp.where(lane >= d, pltpu.roll(x, d, 1), 0.0)
        d *= 2
    tot = jnp.broadcast_to(x[:, LANES - 1:LANES], x.shape)
    row = lax.broadcasted_iota(jnp.int32, x.shape, 0) % rows_per_seq
    inc = tot
    d = 1
    while d < rows_per_seq:
        inc = inc + jnp.where(row >= d, pltpu.roll(inc, d, 0), 0.0)
        d *= 2
    o_ref[...] = x + (inc - tot)


def _decay_scan(lf_rows, rows_per_seq):
    return pl.pallas_call(
        functools.partial(_decay_scan_kernel, rows_per_seq=rows_per_seq),
        out_shape=jax.ShapeDtypeStruct(lf_rows.shape, jnp.float32),
        compiler_params=pltpu.CompilerParams(vmem_limit_bytes=VMEM_LIMIT),
        name="decay_scan",
    )(lf_rows)


def _ones_rows(t):
    return jnp.where(lax.broadcasted_iota(jnp.int32, (SUM_ROWS, t), 0) == 0, 1.0, 0.0
                     ).astype(jnp.bfloat16)


def _flash_units(n_groups, t, kinds, score_fn, value_fn, s_bufs, m_ref, acc_ref):
    units = [(g, slice(c * t, (c + 1) * t), kind) for g in range(n_groups)
             for c, kind in enumerate(kinds) if kind is not None]

    def produce(n):
        g, cols, kind = units[n]
        s = score_fn(g, cols, kind)
        s_bufs[n % 2][...] = s
        return jnp.max(s, axis=0, keepdims=True)

    mx = produce(0)
    for n, (g, cols, _) in enumerate(units):
        mx_next = produce(n + 1) if n + 1 < len(units) else None
        m_prev = m_ref[g, :, cols]
        m_new = jnp.maximum(m_prev, mx)
        m_ref[g, :, cols] = m_new
        p = jnp.exp2(s_bufs[n % 2][...] - m_new).astype(jnp.bfloat16)
        pv = jnp.dot(value_fn(g), p, preferred_element_type=jnp.float32)
        acc_ref[g, :, cols] = jnp.exp2(m_prev - m_new) * acc_ref[g, :, cols] + pv
        mx = mx_next


def _init_stats(m_ref, acc_ref):
    m_ref[...] = jnp.full(m_ref.shape, NEG_BIG, jnp.float32)
    acc_ref[...] = jnp.zeros(acc_ref.shape, jnp.float32)


def _fox_kernel(i_ref, j_ref, j0_ref, qt_ref, vt_ref, k_ref, dec_ref, o_ref,
                s0_ref, s1_ref, m_ref, acc_ref):
    p_id = pl.program_id(1)
    t = s0_ref.shape[0]
    parts = qt_ref.shape[1] // t
    rel = j_ref[p_id] - parts * i_ref[p_id]
    j = j_ref[p_id]
    j0 = j0_ref[pl.program_id(0) * pl.num_programs(1) + p_id]

    @pl.when(rel == parts - 1)
    def _():
        _init_stats(m_ref, acc_ref)

    ones = _ones_rows(t)
    zeros = jnp.zeros((HEAD_DIM, t), jnp.bfloat16)
    dec_row = lax.broadcasted_iota(jnp.int32, (LANES, t), 0)

    def rows(h):
        return slice(h * HEAD_DIM, (h + 1) * HEAD_DIM)

    def scores(h, cols, kind):
        q = qt_ref[rows(h), cols]
        pick = ((dec_row >= N_DECAY_PARTS * h) & (dec_row < N_DECAY_PARTS * (h + 1)))
        qa = jnp.concatenate(([q, zeros] if h % 2 == 0 else [zeros, q])
                             + [jnp.where(pick, 1.0, 0.0).astype(jnp.bfloat16)], axis=0)
        pair = slice((h // 2) * LANES, (h // 2 + 1) * LANES)
        ka = jnp.concatenate([k_ref[:, pair], dec_ref[0]], axis=1)
        s = jnp.dot(ka, qa, preferred_element_type=jnp.float32)
        if kind == "diag":
            kpos = lax.broadcasted_iota(jnp.int32, s.shape, 0)
            qpos = lax.broadcasted_iota(jnp.int32, s.shape, 1)
            s = jnp.where(kpos <= qpos, s, NEG_BIG)
        return s

    def values(h):
        return jnp.concatenate([vt_ref[rows(h), :], ones], axis=0)

    def sweep(kinds):
        _flash_units(FOX_HEADS, t, kinds, scores, values, (s0_ref, s1_ref), m_ref, acc_ref)

    for r in range(parts):
        @pl.when(rel == r)
        def _(r=r):
            sweep([None] * r + ["diag"] + ["past"] * (parts - 1 - r))

    @pl.when((rel < 0) & (j >= j0))
    def _():
        sweep(["past"] * parts)

    @pl.when(j == 0)
    def _():
        for h in range(FOX_HEADS):
            a = acc_ref[h]
            o_ref[rows(h), :] = (a[:HEAD_DIM] / a[HEAD_DIM:HEAD_DIM + 1]).astype(o_ref.dtype)


def _fox_attn(zt, zk, dec, pairs_i, pairs_j, first_tile, batch, seq, t, parts):
    nt = seq // t
    tq = parts * t
    nq = seq // tq
    n = batch * seq
    n_pairs = pairs_i.shape[0]

    def key_tile(b, p, jj, j0):
        return jnp.maximum(jj[p], j0[b * n_pairs + p])

    grid_spec = pltpu.PrefetchScalarGridSpec(
        num_scalar_prefetch=3,
        grid=(batch, n_pairs),
        in_specs=[
            pl.BlockSpec((FOX_WIDTH, tq), lambda b, p, ii, jj, j0: (0, b * nq + ii[p])),
            pl.BlockSpec((FOX_WIDTH, t),
                         lambda b, p, ii, jj, j0: (1, b * nt + key_tile(b, p, jj, j0))),
            pl.BlockSpec((t, FOX_WIDTH),
                         lambda b, p, ii, jj, j0: (b * nt + key_tile(b, p, jj, j0), 0)),
            pl.BlockSpec((1, t, LANES), lambda b, p, ii, jj, j0: (b, key_tile(b, p, jj, j0), 0)),
        ],
        out_specs=pl.BlockSpec((FOX_WIDTH, tq), lambda b, p, ii, jj, j0: (0, b * nq + ii[p])),
        scratch_shapes=[
            pltpu.VMEM((t, t), jnp.float32),
            pltpu.VMEM((t, t), jnp.float32),
            pltpu.VMEM((FOX_HEADS, 1, tq), jnp.float32),
            pltpu.VMEM((FOX_HEADS, HEAD_DIM + SUM_ROWS, tq), jnp.float32),
        ],
    )
    return pl.pallas_call(
        _fox_kernel,
        grid_spec=grid_spec,
        out_shape=jax.ShapeDtypeStruct((FOX_WIDTH, n), jnp.bfloat16),
        compiler_params=pltpu.CompilerParams(
            dimension_semantics=("arbitrary", "arbitrary"), vmem_limit_bytes=VMEM_LIMIT),
        name="fox_attn",
    )(pairs_i, pairs_j, first_tile, zt, zt, zk, dec)


def _fox_first_tiles(zt, zk, pieces, pairs_i, batch, seq, t, parts):
    nt = seq // t
    nq = nt // parts
    f32 = jnp.float32
    q = zt[:FOX_WIDTH].astype(f32).reshape(FOX_HEADS, HEAD_DIM, batch, nq, parts * t)
    aq = jnp.sqrt(jnp.max(jnp.sum(q * q, axis=1), axis=-1)).transpose(1, 2, 0)
    k = zk[:, :FOX_WIDTH].astype(f32).reshape(batch, nt, t, FOX_HEADS, HEAD_DIM)
    ak = jnp.sqrt(jnp.max(jnp.sum(k * k, axis=-1), axis=2))
    aq = aq * NORM_SLACK
    ak = ak * NORM_SLACK
    d = sum(p.astype(f32) for p in pieces).reshape(batch, FOX_HEADS, nt, t)
    d_first = d[:, :, ::parts, 0].transpose(0, 2, 1)
    d_last = d[..., t - 1].transpose(0, 2, 1)
    own_k = jnp.max(ak.reshape(batch, nq, parts, FOX_HEADS), axis=2)
    row_floor = d_first - aq * own_k
    tile_top = aq[:, :, None, :] * ak[:, None, :, :] + d_last[:, None, :, :]
    alive = jnp.any(tile_top >= row_floor[:, :, None, :] - DEAD_GAP, axis=-1)
    alive = alive | (jnp.arange(nt)[None, None, :] >= parts * jnp.arange(nq)[None, :, None])
    first = jnp.argmax(alive, axis=-1).astype(jnp.int32)
    return first[:, pairs_i].reshape(-1)


def _diff_kernel(i_ref, j_ref, qt_ref, vt_ref, kd_ref, bd_ref, bs_ref, lam_ref, g_ref,
                 o_ref, s0_ref, s1_ref, m_ref, acc_ref, *, lambda_init):
    p_id = pl.program_id(1)
    t = s0_ref.shape[0]
    parts = qt_ref.shape[1] // t
    j = j_ref[p_id]
    rel = j - parts * i_ref[p_id]

    @pl.when(j == 0)
    def _():
        _init_stats(m_ref, acc_ref)

    first_map = lax.broadcasted_iota(jnp.int32, (DIFF_V_DIM, t), 0) < HEAD_DIM
    ones = _ones_rows(t)

    def rows(h):
        return slice(h * DIFF_V_DIM, (h + 1) * DIFF_V_DIM)

    def scores(u, cols, kind):
        h, mp = divmod(u, 2)
        q = qt_ref[rows(h), cols]
        zero = jnp.zeros_like(q)
        qm = jnp.where(first_map, q, zero) if mp == 0 else jnp.where(first_map, zero, q)
        kd = kd_ref[:, rows(h)]
        if kind == "sub":
            s_far = jnp.dot(kd[:t - LANES], qm, preferred_element_type=jnp.float32)
            s_near = jnp.dot(kd[t - LANES:], qm, preferred_element_type=jnp.float32)
            return jnp.concatenate([s_far, s_near + bs_ref[h]], axis=0)
        s = jnp.dot(kd, qm, preferred_element_type=jnp.float32)
        if kind == "diag":
            s = s + bd_ref[h]
        return s

    def values(u):
        return jnp.concatenate([vt_ref[rows(u // 2), :], ones], axis=0)

    def kind_of(d):
        return None if d > 0 else "diag" if d == 0 else "sub" if d == -1 else "far"

    def sweep(r):
        kinds = [kind_of(r - c) for c in range(parts)]
        _flash_units(2 * DIFF_HEADS, t, kinds, scores, values, (s0_ref, s1_ref), m_ref, acc_ref)

    @pl.when(rel < -1)
    def _():
        sweep(-2)

    for r in range(-1, parts):
        @pl.when(rel == r)
        def _(r=r):
            sweep(r)

    @pl.when(rel == parts - 1)
    def _():
        lam_rows = lam_ref[...]
        dot1 = jnp.sum(lam_rows[0:1] * lam_rows[1:2], axis=-1, keepdims=True)
        dot2 = jnp.sum(lam_rows[2:3] * lam_rows[3:4], axis=-1, keepdims=True)
        lam = jnp.exp(dot1) - jnp.exp(dot2) + lambda_init
        g = g_ref[...] * (1.0 - lambda_init)
        for h in range(DIFF_HEADS):
            a1 = acc_ref[2 * h]
            a2 = acc_ref[2 * h + 1]
            d = (a1[:DIFF_V_DIM] / a1[DIFF_V_DIM:DIFF_V_DIM + 1]
                 - lam * (a2[:DIFF_V_DIM] / a2[DIFF_V_DIM:DIFF_V_DIM + 1]))
            y = d * lax.rsqrt(jnp.mean(d * d, axis=0, keepdims=True) + SUBLN_EPS)
            o_ref[rows(h), :] = (y * g).astype(o_ref.dtype)


def _diff_attn(zt, kd, bias_diag, bias_sub, lam_rows, g_col, pairs_i, pairs_j,
               batch, seq, t, parts, lambda_init):
    nt = seq // t
    tq = parts * t
    nq = seq // tq
    n = batch * seq
    grid_spec = pltpu.PrefetchScalarGridSpec(
        num_scalar_prefetch=2,
        grid=(batch, pairs_i.shape[0]),
        in_specs=[
            pl.BlockSpec((DIFF_WIDTH, tq), lambda b, p, ii, jj: (2, b * nq + ii[p])),
            pl.BlockSpec((DIFF_WIDTH, t), lambda b, p, ii, jj: (3, b * nt + jj[p])),
            pl.BlockSpec((t, DIFF_WIDTH), lambda b, p, ii, jj: (b * nt + jj[p], 1)),
            pl.BlockSpec((DIFF_HEADS, t, t), lambda b, p, ii, jj: (0, 0, 0)),
            pl.BlockSpec((DIFF_HEADS, LANES, t), lambda b, p, ii, jj: (0, 0, 0)),
            pl.BlockSpec((8, LANES), lambda b, p, ii, jj: (0, 0)),
            pl.BlockSpec((DIFF_V_DIM, 1), lambda b, p, ii, jj: (0, 0)),
        ],
        out_specs=pl.BlockSpec((DIFF_WIDTH, tq), lambda b, p, ii, jj: (0, b * nq + ii[p])),
        scratch_shapes=[
            pltpu.VMEM((t, t), jnp.float32),
            pltpu.VMEM((t, t), jnp.float32),
            pltpu.VMEM((2 * DIFF_HEADS, 1, tq), jnp.float32),
            pltpu.VMEM((2 * DIFF_HEADS, DIFF_V_DIM + SUM_ROWS, tq), jnp.float32),
        ],
    )
    return pl.pallas_call(
        functools.partial(_diff_kernel, lambda_init=lambda_init),
        grid_spec=grid_spec,
        out_shape=jax.ShapeDtypeStruct((DIFF_WIDTH, n), jnp.bfloat16),
        compiler_params=pltpu.CompilerParams(
            dimension_semantics=("arbitrary", "arbitrary"), vmem_limit_bytes=VMEM_LIMIT),
        name="diff_attn",
    )(pairs_i, pairs_j, zt, zt, kd, bias_diag, bias_sub, lam_rows, g_col)


def _out_mlp_kernel(x_ref, fox_ref, dif_ref, wo_ref, g_ref, w1_ref, w2_ref, gf_ref, o_ref,
                    *, final_norm):
    y = lax.dot_general(fox_ref[...], wo_ref[:FOX_WIDTH, :], _TN,
                        preferred_element_type=jnp.float32)
    y = y + lax.dot_general(dif_ref[...], wo_ref[FOX_WIDTH:, :], _TN,
                            preferred_element_type=jnp.float32)
    x1 = x_ref[...] + y
    h = _rms(x1, g_ref[...], NORM_EPS).astype(jnp.bfloat16)
    acc = x1
    for c in range(D_FF // FF_CHUNK):
        u = jnp.dot(h, w1_ref[:, c * FF_CHUNK:(c + 1) * FF_CHUNK],
                    preferred_element_type=jnp.float32)
        u = jnp.square(jnp.maximum(u, 0.0)).astype(jnp.bfloat16)
        acc = acc + jnp.dot(u, w2_ref[c * FF_CHUNK:(c + 1) * FF_CHUNK, :],
                            preferred_element_type=jnp.float32)
    if final_norm:
        acc = _rms(acc, gf_ref[...], NORM_EPS)
    o_ref[...] = acc


def _out_mlp(x2d, fox_t, dif_t, wo, g, w1, w2, gf, tm, final_norm):
    n = x2d.shape[0]
    const = lambda i: (0, 0)
    return pl.pallas_call(
        functools.partial(_out_mlp_kernel, final_norm=final_norm),
        grid=(n // tm,),
        in_specs=[
            pl.BlockSpec((tm, D_MODEL), lambda i: (i, 0)),
            pl.BlockSpec((FOX_WIDTH, tm), lambda i: (0, i)),
            pl.BlockSpec((DIFF_WIDTH, tm), lambda i: (0, i)),
            pl.BlockSpec((D_MODEL, D_MODEL), const, pipeline_mode=pl.Buffered(1)),
            pl.BlockSpec((1, D_MODEL), const),
            pl.BlockSpec((D_MODEL, D_FF), const, pipeline_mode=pl.Buffered(1)),
            pl.BlockSpec((D_FF, D_MODEL), const, pipeline_mode=pl.Buffered(1)),
            pl.BlockSpec((1, D_MODEL), const),
        ],
        out_specs=pl.BlockSpec((tm, D_MODEL), lambda i: (i, 0)),
        out_shape=jax.ShapeDtypeStruct((n, D_MODEL), jnp.float32),
        compiler_params=pltpu.CompilerParams(
            dimension_semantics=("arbitrary",), vmem_limit_bytes=VMEM_LIMIT),
        name="out_mlp",
    )(x2d, fox_t, dif_t, wo, g, w1, w2, gf)


def _t5_bucket(rel):
    half = N_BUCKETS // 2
    max_exact = half // 2
    ret = jnp.where(rel > 0, half, 0)
    n = jnp.abs(rel)
    nf = jnp.maximum(n, 1).astype(jnp.float32)
    large = max_exact + (jnp.log(nf / max_exact) / math.log(MAX_DISTANCE / max_exact)
                         * (half - max_exact)).astype(jnp.int32)
    large = jnp.minimum(large, half - 1)
    return ret + jnp.where(n < max_exact, n, large)


def _bias_tiles(table, t):
    tab = table.astype(jnp.float32)

    def lookup(rel):
        bucket = _t5_bucket(rel)
        out = jnp.zeros((tab.shape[1],) + rel.shape, jnp.float32)
        for b in range(N_BUCKETS):
            out = jnp.where(bucket == b, tab[b].reshape((-1,) + (1,) * rel.ndim), out)
        return out

    far = lookup(jnp.full((1, 1), -MAX_DISTANCE, jnp.int32))
    k = jnp.arange(t)[:, None]
    q = jnp.arange(t)[None, :]
    rel = k - q
    diag = (lookup(rel) - far) * LOG2E
    diag = jnp.where((k // CHUNK) <= (q // CHUNK), diag, NEG_BIG)
    rel_sub = rel[t - LANES:] - t
    sub = (lookup(rel_sub) - far) * LOG2E
    return diag, sub


def _split_bf16(x, parts):
    out = []
    for _ in range(parts - 1):
        bits = lax.bitcast_convert_type(x, jnp.uint32) & jnp.uint32(0xFFFF0000)
        piece = lax.bitcast_convert_type(bits, jnp.float32)
        out.append(piece.astype(jnp.bfloat16))
        x = x - piece
    out.append(x.astype(jnp.bfloat16))
    return out


def _forward(x, w_in, b_f, lambda_q1, lambda_k1, lambda_q2, lambda_k2, subln_g, w_out,
             norm_attn_g, norm_mlp_g, w_mlp_in, w_mlp_out, rel_bias_table, final_norm_g,
             *, attn_tile, row_tile):
    batch, seq, _ = x.shape
    depth = w_in.shape[0]
    n = batch * seq
    t = attn_tile
    tm = min(row_tile, n)
    qp, dp = FOX_Q_PARTS, DIFF_Q_PARTS
    assert seq % (qp * t) == 0 and seq % (dp * t) == 0 and n % tm == 0
    assert t % LANES == 0 and t > LANES
    pairs = [(i, j) for i in range(seq // (dp * t)) for j in range(dp * (i + 1))]
    pairs_i = jnp.asarray([p[0] for p in pairs], jnp.int32)
    pairs_j = jnp.asarray([p[1] for p in pairs], jnp.int32)
    fox_pairs = [(i, j) for i in range(seq // (qp * t))
                 for j in range(qp * (i + 1) - 1, -1, -1)]
    fox_i = jnp.asarray([p[0] for p in fox_pairs], jnp.int32)
    fox_j = jnp.asarray([p[1] for p in fox_pairs], jnp.int32)
    bias_diag, bias_sub = _bias_tiles(rel_bias_table, t)

    o_fq, o_fk, o_fv = 0, FOX_WIDTH, 2 * FOX_WIDTH
    o_ff = 3 * FOX_WIDTH
    o_dq = o_ff + FOX_HEADS
    o_dk = o_dq + DIFF_WIDTH
    o_dv = o_dk + DIFF_WIDTH
    qscale = LOG2E / math.sqrt(HEAD_DIM)

    x2d = x.reshape(n, D_MODEL)
    for l in range(depth):
        w = w_in[l]
        wqv_t = jnp.concatenate([
            w[:, o_fq:o_fq + FOX_WIDTH] * qscale, w[:, o_fv:o_fv + FOX_WIDTH],
            w[:, o_dq:o_dq + DIFF_WIDTH] * qscale, w[:, o_dv:o_dv + DIFF_WIDTH]],
            axis=1).T.astype(jnp.bfloat16)
        wk = jnp.concatenate([w[:, o_fk:o_fk + FOX_WIDTH], w[:, o_dk:o_dk + DIFF_WIDTH]],
                             axis=1).astype(jnp.bfloat16)
        wf = jnp.pad(w[:, o_ff:o_ff + FOX_HEADS], ((0, 0), (0, LANES - FOX_HEADS))
                     ).astype(jnp.bfloat16)
        bf = jnp.pad(b_f[l].astype(jnp.float32), (0, LANES - FOX_HEADS)).reshape(1, LANES)

        zt, zk, lf = _in_proj(x2d, norm_attn_g[l].reshape(1, D_MODEL), wqv_t, wk, wf, bf, tm)

        lf_rows = lf[:, :FOX_HEADS].reshape(batch, seq, FOX_HEADS).transpose(0, 2, 1)
        c = _decay_scan(lf_rows.reshape(batch * FOX_HEADS * seq // LANES, LANES), seq // LANES)
        c = c.reshape(batch, FOX_HEADS, seq)
        parts = _split_bf16(c * (-LOG2E), N_DECAY_PARTS)
        dec = jnp.stack(parts, axis=2).transpose(0, 3, 1, 2)
        dec = jnp.pad(dec.reshape(batch, seq, FOX_HEADS * N_DECAY_PARTS),
                      ((0, 0), (0, 0), (0, LANES - FOX_HEADS * N_DECAY_PARTS)))

        first_tile = _fox_first_tiles(zt, zk, parts, fox_i, batch, seq, t, qp)
        fox_t = _fox_attn(zt, zk, dec, fox_i, fox_j, first_tile, batch, seq, t, qp)

        lambda_init = 0.8 - 0.6 * math.exp(-0.3 * l)
        lam_rows = jnp.pad(
            jnp.stack([lambda_q1[l], lambda_k1[l], lambda_q2[l], lambda_k2[l]]).astype(jnp.float32),
            ((0, 4), (0, LANES - HEAD_DIM)))
        dif_t = _diff_attn(zt, zk, bias_diag, bias_sub, lam_rows,
                           subln_g[l].astype(jnp.float32).reshape(DIFF_V_DIM, 1),
                           pairs_i, pairs_j, batch, seq, t, dp, lambda_init)

        x2d = _out_mlp(x2d, fox_t, dif_t, w_out[l].astype(jnp.bfloat16),
                       norm_mlp_g[l].reshape(1, D_MODEL), w_mlp_in[l].astype(jnp.bfloat16),
                       w_mlp_out[l].astype(jnp.bfloat16), final_norm_g.reshape(1, D_MODEL),
                       tm, final_norm=(l == depth - 1))
    return x2d.reshape(batch, seq, D_MODEL)


def kernel(x, w_in, b_f, lambda_q1, lambda_k1, lambda_q2, lambda_k2, subln_g, w_out,
           norm_attn_g, norm_mlp_g, w_mlp_in, w_mlp_out, rel_bias_table, final_norm_g):
    return _forward(x, w_in, b_f, lambda_q1, lambda_k1, lambda_q2, lambda_k2, subln_g, w_out,
                    norm_attn_g, norm_mlp_g, w_mlp_in, w_mlp_out, rel_bias_table, final_norm_g,
                    attn_tile=ATTN_TILE, row_tile=ROW_TILE)
```

```python
import functools
import math

import jax
import jax.numpy as jnp
import numpy as np
from jax import lax
from jax.experimental import pallas as pl
from jax.experimental.pallas import tpu as pltpu

D_MODEL = 1024
CHUNK = 64
FOX_HEADS = 8
HEAD_DIM = 64
FOX_WIDTH = FOX_HEADS * HEAD_DIM
DIFF_HEADS = 4
DIFF_V_DIM = 2 * HEAD_DIM
DIFF_WIDTH = DIFF_HEADS * DIFF_V_DIM
D_FF = 4 * D_MODEL
N_BUCKETS = 32
MAX_DISTANCE = 128
NORM_EPS = 1e-5
SUBLN_EPS = 1e-5

LANES = 128
LOG2E = math.log2(math.e)
NEG_BIG = -1e30
N_DECAY_PARTS = 3
SUM_ROWS = 16
FOX_Q_PARTS = 2
DIFF_Q_PARTS = 2
DEAD_GAP = 152.0
NORM_SLACK = 1.001
ATTN_TILE = 512
ROW_TILE = 512
FF_CHUNK = 1024
VMEM_LIMIT = 48 * 1024 * 1024

_NT = (((1,), (1,)), ((), ()))
_TN = (((0,), (0,)), ((), ()))


def _rms(x, g, eps):
    return x * lax.rsqrt(jnp.mean(x * x, axis=-1, keepdims=True) + eps) * g


def _in_proj_kernel(x_ref, g_ref, wqv_ref, wk_ref, wf_ref, bf_ref, ind_ref,
                    zt_ref, zk_ref, lf_ref, st_ref):
    h = _rms(x_ref[...], g_ref[...], NORM_EPS).astype(jnp.bfloat16)
    zt = lax.dot_general(wqv_ref[...], h, _NT, preferred_element_type=jnp.float32)
    zt = zt.astype(jnp.bfloat16)
    zt_ref[...] = zt
    zk = jnp.dot(h, wk_ref[...], preferred_element_type=jnp.float32).astype(jnp.bfloat16)
    zk_ref[...] = zk
    f = jnp.dot(h, wf_ref[...], preferred_element_type=jnp.float32) + bf_ref[...]
    lf_ref[...] = jnp.minimum(f, 0.0) - jnp.log1p(jnp.exp(-jnp.abs(f)))
    tm = zt.shape[1]
    q = zt[:FOX_WIDTH].astype(jnp.float32)
    qn = jnp.sum((q * q).reshape(FOX_HEADS, HEAD_DIM, tm), axis=1)
    k = zk[:, :FOX_WIDTH].astype(jnp.float32)
    kn = jnp.dot(k * k, ind_ref[...], precision=lax.Precision.HIGHEST,
                 preferred_element_type=jnp.float32)
    st_ref[0, :FOX_HEADS, :] = jnp.broadcast_to(jnp.max(qn, axis=1, keepdims=True),
                                                (FOX_HEADS, LANES))
    st_ref[0, FOX_HEADS:, :] = jnp.broadcast_to(jnp.max(kn, axis=0, keepdims=True),
                                                (FOX_HEADS, LANES))


def _in_proj(x2d, g, wqv_t, wk, wf, bf, tm):
    n = x2d.shape[0]
    nqv = wqv_t.shape[0]
    nk = wk.shape[1]
    const = lambda i: (0, 0)
    head_of_col = jnp.arange(FOX_WIDTH)[:, None] // HEAD_DIM == jnp.arange(LANES)[None, :]
    return pl.pallas_call(
        _in_proj_kernel,
        grid=(n // tm,),
        in_specs=[
            pl.BlockSpec((tm, D_MODEL), lambda i: (i, 0)),
            pl.BlockSpec((1, D_MODEL), const),
            pl.BlockSpec((nqv, D_MODEL), const),
            pl.BlockSpec((D_MODEL, nk), const),
            pl.BlockSpec((D_MODEL, LANES), const),
            pl.BlockSpec((1, LANES), const),
            pl.BlockSpec((FOX_WIDTH, LANES), const),
        ],
        out_specs=[
            pl.BlockSpec((nqv, tm), lambda i: (0, i)),
            pl.BlockSpec((tm, nk), lambda i: (i, 0)),
            pl.BlockSpec((tm, LANES), lambda i: (i, 0)),
            pl.BlockSpec((1, 2 * FOX_HEADS, LANES), lambda i: (i, 0, 0)),
        ],
        out_shape=[
            jax.ShapeDtypeStruct((nqv, n), jnp.bfloat16),
            jax.ShapeDtypeStruct((n, nk), jnp.bfloat16),
            jax.ShapeDtypeStruct((n, LANES), jnp.float32),
            jax.ShapeDtypeStruct((n // tm, 2 * FOX_HEADS, LANES), jnp.float32),
        ],
        compiler_params=pltpu.CompilerParams(
            dimension_semantics=("arbitrary",), vmem_limit_bytes=VMEM_LIMIT),
        name="in_proj",
    )(x2d, g, wqv_t, wk, wf, bf, head_of_col.astype(jnp.float32))


def _decay_scan_kernel(x_ref, o_ref, *, rows_per_seq):
    x = x_ref[...]
    lane = lax.broadcasted_iota(jnp.int32, x.shape, 1)
    d = 1
    while d < LANES:
        x = x + jnp.where(lane >= d, pltpu.roll(x, d, 1), 0.0)
        d *= 2
    tot = jnp.broadcast_to(x[:, LANES - 1:LANES], x.shape)
    row = lax.broadcasted_iota(jnp.int32, x.shape, 0) % rows_per_seq
    inc = tot
    d = 1
    while d < rows_per_seq:
        inc = inc + jnp.where(row >= d, pltpu.roll(inc, d, 0), 0.0)
        d *= 2
    o_ref[...] = x + (inc - tot)


def _decay_scan(lf_rows, rows_per_seq):
    return pl.pallas_call(
        functools.partial(_decay_scan_kernel, rows_per_seq=rows_per_seq),
        out_shape=jax.ShapeDtypeStruct(lf_rows.shape, jnp.float32),
        compiler_params=pltpu.CompilerParams(vmem_limit_bytes=VMEM_LIMIT),
        name="decay_scan",
    )(lf_rows)


def _ones_rows(t):
    return jnp.where(lax.broadcasted_iota(jnp.int32, (SUM_ROWS, t), 0) == 0, 1.0, 0.0
                     ).astype(jnp.bfloat16)


def _flash_units(n_groups, t, kinds, score_fn, value_fn, s_bufs, m_ref, acc_ref):
    units = [(g, slice(c * t, (c + 1) * t), kind) for g in range(n_groups)
             for c, kind in enumerate(kinds) if kind is not None]

    def produce(n):
        g, cols, kind = units[n]
        s = score_fn(g, cols, kind)
        s_bufs[n % 2][...] = s
        return jnp.max(s, axis=0, keepdims=True)

    mx = produce(0)
    for n, (g, cols, _) in enumerate(units):
        mx_next = produce(n + 1) if n + 1 < len(units) else None
        m_prev = m_ref[g, :, cols]
        m_new = jnp.maximum(m_prev, mx)
        m_ref[g, :, cols] = m_new
        p = jnp.exp2(s_bufs[n % 2][...] - m_new).astype(jnp.bfloat16)
        pv = jnp.dot(value_fn(g), p, preferred_element_type=jnp.float32)
        acc_ref[g, :, cols] = jnp.exp2(m_prev - m_new) * acc_ref[g, :, cols] + pv
        mx = mx_next


def _init_stats(m_ref, acc_ref):
    m_ref[...] = jnp.full(m_ref.shape, NEG_BIG, jnp.float32)
    acc_ref[...] = jnp.zeros(acc_ref.shape, jnp.float32)


def _fox_kernel(i_ref, j_ref, j0_ref, qt_ref, vt_ref, k_ref, dec_ref, o_ref,
                s0_ref, s1_ref, m_ref, acc_ref):
    p_id = pl.program_id(1)
    t = s0_ref.shape[0]
    parts = qt_ref.shape[1] // t
    rel = j_ref[p_id] - parts * i_ref[p_id]
    j = j_ref[p_id]
    j0 = j0_ref[pl.program_id(0) * pl.num_programs(1) + p_id]

    @pl.when(rel == parts - 1)
    def _():
        _init_stats(m_ref, acc_ref)

    ones = _ones_rows(t)
    zeros = jnp.zeros((HEAD_DIM, t), jnp.bfloat16)
    dec_row = lax.broadcasted_iota(jnp.int32, (LANES, t), 0)

    def rows(h):
        return slice(h * HEAD_DIM, (h + 1) * HEAD_DIM)

    def scores(h, cols, kind):
        q = qt_ref[rows(h), cols]
        pick = ((dec_row >= N_DECAY_PARTS * h) & (dec_row < N_DECAY_PARTS * (h + 1)))
        qa = jnp.concatenate(([q, zeros] if h % 2 == 0 else [zeros, q])
                             + [jnp.where(pick, 1.0, 0.0).astype(jnp.bfloat16)], axis=0)
        pair = slice((h // 2) * LANES, (h // 2 + 1) * LANES)
        ka = jnp.concatenate([k_ref[:, pair], dec_ref[0]], axis=1)
        s = jnp.dot(ka, qa, preferred_element_type=jnp.float32)
        if kind == "diag":
            kpos = lax.broadcasted_iota(jnp.int32, s.shape, 0)
            qpos = lax.broadcasted_iota(jnp.int32, s.shape, 1)
            s = jnp.where(kpos <= qpos, s, NEG_BIG)
        return s

    def values(h):
        return jnp.concatenate([vt_ref[rows(h), :], ones], axis=0)

    def sweep(kinds):
        _flash_units(FOX_HEADS, t, kinds, scores, values, (s0_ref, s1_ref), m_ref, acc_ref)

    for r in range(parts):
        @pl.when(rel == r)
        def _(r=r):
            sweep([None] * r + ["diag"] + ["past"] * (parts - 1 - r))

    @pl.when((rel < 0) & (j >= j0))
    def _():
        sweep(["past"] * parts)

    @pl.when(j == 0)
    def _():
        for h in range(FOX_HEADS):
            a = acc_ref[h]
            o_ref[rows(h), :] = (a[:HEAD_DIM] / a[HEAD_DIM:HEAD_DIM + 1]).astype(o_ref.dtype)


def _fox_attn(zt, zk, dec, pairs_i, pairs_j, first_tile, batch, seq, t, parts):
    nt = seq // t
    tq = parts * t
    nq = seq // tq
    n = batch * seq
    n_pairs = pairs_i.shape[0]

    def key_tile(b, p, jj, j0):
        return jnp.maximum(jj[p], j0[b * n_pairs + p])

    grid_spec = pltpu.PrefetchScalarGridSpec(
        num_scalar_prefetch=3,
        grid=(batch, n_pairs),
        in_specs=[
            pl.BlockSpec((FOX_WIDTH, tq), lambda b, p, ii, jj, j0: (0, b * nq + ii[p])),
            pl.BlockSpec((FOX_WIDTH, t),
                         lambda b, p, ii, jj, j0: (1, b * nt + key_tile(b, p, jj, j0))),
            pl.BlockSpec((t, FOX_WIDTH),
                         lambda b, p, ii, jj, j0: (b * nt + key_tile(b, p, jj, j0), 0)),
            pl.BlockSpec((1, t, LANES), lambda b, p, ii, jj, j0: (b, key_tile(b, p, jj, j0), 0)),
        ],
        out_specs=pl.BlockSpec((FOX_WIDTH, tq), lambda b, p, ii, jj, j0: (0, b * nq + ii[p])),
        scratch_shapes=[
            pltpu.VMEM((t, t), jnp.float32),
            pltpu.VMEM((t, t), jnp.float32),
            pltpu.VMEM((FOX_HEADS, 1, tq), jnp.float32),
            pltpu.VMEM((FOX_HEADS, HEAD_DIM + SUM_ROWS, tq), jnp.float32),
        ],
    )
    return pl.pallas_call(
        _fox_kernel,
        grid_spec=grid_spec,
        out_shape=jax.ShapeDtypeStruct((FOX_WIDTH, n), jnp.bfloat16),
        compiler_params=pltpu.CompilerParams(
            dimension_semantics=("arbitrary", "arbitrary"), vmem_limit_bytes=VMEM_LIMIT),
        name="fox_attn",
    )(pairs_i, pairs_j, first_tile, zt, zt, zk, dec)


def _fox_first_tiles(stats, pieces, pairs_i, batch, seq, t, parts):
    nt = seq // t
    nq = nt // parts
    f32 = jnp.float32
    q2 = stats[:, :FOX_HEADS, 0].reshape(batch, nq, parts, FOX_HEADS)
    aq = jnp.sqrt(jnp.max(q2, axis=2)) * NORM_SLACK
    ak = jnp.sqrt(stats[:, FOX_HEADS, :FOX_HEADS].reshape(batch, nt, FOX_HEADS)) * NORM_SLACK
    d = sum(p.astype(f32) for p in pieces).reshape(batch, FOX_HEADS, nt, t)
    d_first = d[:, :, ::parts, 0].transpose(0, 2, 1)
    d_last = d[..., t - 1].transpose(0, 2, 1)
    own_k = jnp.max(ak.reshape(batch, nq, parts, FOX_HEADS), axis=2)
    row_floor = d_first - aq * own_k
    tile_top = aq[:, :, None, :] * ak[:, None, :, :] + d_last[:, None, :, :]
    alive = jnp.any(tile_top >= row_floor[:, :, None, :] - DEAD_GAP, axis=-1)
    alive = alive | (jnp.arange(nt)[None, None, :] >= parts * jnp.arange(nq)[None, :, None])
    first = jnp.argmax(alive, axis=-1).astype(jnp.int32)
    return first[:, pairs_i].reshape(-1)


def _diff_kernel(i_ref, j_ref, qt_ref, vt_ref, kd_ref, bd_ref, bs_ref, lam_ref, g_ref,
                 o_ref, s0_ref, s1_ref, m_ref, acc_ref, *, lambda_init):
    p_id = pl.program_id(1)
    t = s0_ref.shape[0]
    parts = qt_ref.shape[1] // t
    j = j_ref[p_id]
    rel = j - parts * i_ref[p_id]

    @pl.when(j == 0)
    def _():
        _init_stats(m_ref, acc_ref)

    first_map = lax.broadcasted_iota(jnp.int32, (DIFF_V_DIM, t), 0) < HEAD_DIM
    ones = _ones_rows(t)

    def rows(h):
        return slice(h * DIFF_V_DIM, (h + 1) * DIFF_V_DIM)

    def scores(u, cols, kind):
        h, mp = divmod(u, 2)
        q = qt_ref[rows(h), cols]
        zero = jnp.zeros_like(q)
        qm = jnp.where(first_map, q, zero) if mp == 0 else jnp.where(first_map, zero, q)
        kd = kd_ref[:, rows(h)]
        if kind == "sub":
            s_far = jnp.dot(kd[:t - LANES], qm, preferred_element_type=jnp.float32)
            s_near = jnp.dot(kd[t - LANES:], qm, preferred_element_type=jnp.float32)
            return jnp.concatenate([s_far, s_near + bs_ref[h]], axis=0)
        s = jnp.dot(kd, qm, preferred_element_type=jnp.float32)
        if kind == "diag":
            s = s + bd_ref[h]
        return s

    def values(u):
        return jnp.concatenate([vt_ref[rows(u // 2), :], ones], axis=0)

    def kind_of(d):
        return None if d > 0 else "diag" if d == 0 else "sub" if d == -1 else "far"

    def sweep(r):
        kinds = [kind_of(r - c) for c in range(parts)]
        _flash_units(2 * DIFF_HEADS, t, kinds, scores, values, (s0_ref, s1_ref), m_ref, acc_ref)

    @pl.when(rel < -1)
    def _():
        sweep(-2)

    for r in range(-1, parts):
        @pl.when(rel == r)
        def _(r=r):
            sweep(r)

    @pl.when(rel == parts - 1)
    def _():
        lam_rows = lam_ref[...]
        dot1 = jnp.sum(lam_rows[0:1] * lam_rows[1:2], axis=-1, keepdims=True)
        dot2 = jnp.sum(lam_rows[2:3] * lam_rows[3:4], axis=-1, keepdims=True)
        lam = jnp.exp(dot1) - jnp.exp(dot2) + lambda_init
        g = g_ref[...] * (1.0 - lambda_init)
        for h in range(DIFF_HEADS):
            a1 = acc_ref[2 * h]
            a2 = acc_ref[2 * h + 1]
            d = (a1[:DIFF_V_DIM] / a1[DIFF_V_DIM:DIFF_V_DIM + 1]
                 - lam * (a2[:DIFF_V_DIM] / a2[DIFF_V_DIM:DIFF_V_DIM + 1]))
            y = d * lax.rsqrt(jnp.mean(d * d, axis=0, keepdims=True) + SUBLN_EPS)
            o_ref[rows(h), :] = (y * g).astype(o_ref.dtype)


def _diff_attn(zt, kd, bias_diag, bias_sub, lam_rows, g_col, pairs_i, pairs_j,
               batch, seq, t, parts, lambda_init):
    nt = seq // t
    tq = parts * t
    nq = seq // tq
    n = batch * seq
    grid_spec = pltpu.PrefetchScalarGridSpec(
        num_scalar_prefetch=2,
        grid=(batch, pairs_i.shape[0]),
        in_specs=[
            pl.BlockSpec((DIFF_WIDTH, tq), lambda b, p, ii, jj: (2, b * nq + ii[p])),
            pl.BlockSpec((DIFF_WIDTH, t), lambda b, p, ii, jj: (3, b * nt + jj[p])),
            pl.BlockSpec((t, DIFF_WIDTH), lambda b, p, ii, jj: (b * nt + jj[p], 1)),
            pl.BlockSpec((DIFF_HEADS, t, t), lambda b, p, ii, jj: (0, 0, 0)),
            pl.BlockSpec((DIFF_HEADS, LANES, t), lambda b, p, ii, jj: (0, 0, 0)),
            pl.BlockSpec((8, LANES), lambda b, p, ii, jj: (0, 0)),
            pl.BlockSpec((DIFF_V_DIM, 1), lambda b, p, ii, jj: (0, 0)),
        ],
        out_specs=pl.BlockSpec((DIFF_WIDTH, tq), lambda b, p, ii, jj: (0, b * nq + ii[p])),
        scratch_shapes=[
            pltpu.VMEM((t, t), jnp.float32),
            pltpu.VMEM((t, t), jnp.float32),
            pltpu.VMEM((2 * DIFF_HEADS, 1, tq), jnp.float32),
            pltpu.VMEM((2 * DIFF_HEADS, DIFF_V_DIM + SUM_ROWS, tq), jnp.float32),
        ],
    )
    return pl.pallas_call(
        functools.partial(_diff_kernel, lambda_init=lambda_init),
        grid_spec=grid_spec,
        out_shape=jax.ShapeDtypeStruct((DIFF_WIDTH, n), jnp.bfloat16),
        compiler_params=pltpu.CompilerParams(
            dimension_semantics=("arbitrary", "arbitrary"), vmem_limit_bytes=VMEM_LIMIT),
        name="diff_attn",
    )(pairs_i, pairs_j, zt, zt, kd, bias_diag, bias_sub, lam_rows, g_col)


def _out_mlp_kernel(x_ref, fox_ref, dif_ref, wo_ref, g_ref, w1_ref, w2_ref, gf_ref, o_ref,
                    *, final_norm):
    y = lax.dot_general(fox_ref[...], wo_ref[:FOX_WIDTH, :], _TN,
                        preferred_element_type=jnp.float32)
    y = y + lax.dot_general(dif_ref[...], wo_ref[FOX_WIDTH:, :], _TN,
                            preferred_element_type=jnp.float32)
    x1 = x_ref[...] + y
    h = _rms(x1, g_ref[...], NORM_EPS).astype(jnp.bfloat16)
    acc = x1
    for c in range(D_FF // FF_CHUNK):
        u = jnp.dot(h, w1_ref[:, c * FF_CHUNK:(c + 1) * FF_CHUNK],
                    preferred_element_type=jnp.float32)
        u = jnp.square(jnp.maximum(u, 0.0)).astype(jnp.bfloat16)
        acc = acc + jnp.dot(u, w2_ref[c * FF_CHUNK:(c + 1) * FF_CHUNK, :],
                            preferred_element_type=jnp.float32)
    if final_norm:
        acc = _rms(acc, gf_ref[...], NORM_EPS)
    o_ref[...] = acc


def _out_mlp(x2d, fox_t, dif_t, wo, g, w1, w2, gf, tm, final_norm):
    n = x2d.shape[0]
    const = lambda i: (0, 0)
    return pl.pallas_call(
        functools.partial(_out_mlp_kernel, final_norm=final_norm),
        grid=(n // tm,),
        in_specs=[
            pl.BlockSpec((tm, D_MODEL), lambda i: (i, 0)),
            pl.BlockSpec((FOX_WIDTH, tm), lambda i: (0, i)),
            pl.BlockSpec((DIFF_WIDTH, tm), lambda i: (0, i)),
            pl.BlockSpec((D_MODEL, D_MODEL), const, pipeline_mode=pl.Buffered(1)),
            pl.BlockSpec((1, D_MODEL), const),
            pl.BlockSpec((D_MODEL, D_FF), const, pipeline_mode=pl.Buffered(1)),
            pl.BlockSpec((D_FF, D_MODEL), const, pipeline_mode=pl.Buffered(1)),
            pl.BlockSpec((1, D_MODEL), const),
        ],
        out_specs=pl.BlockSpec((tm, D_MODEL), lambda i: (i, 0)),
        out_shape=jax.ShapeDtypeStruct((n, D_MODEL), jnp.float32),
        compiler_params=pltpu.CompilerParams(
            dimension_semantics=("arbitrary",), vmem_limit_bytes=VMEM_LIMIT),
        name="out_mlp",
    )(x2d, fox_t, dif_t, wo, g, w1, w2, gf)


def _t5_bucket(rel):
    half = N_BUCKETS // 2
    max_exact = half // 2
    ret = jnp.where(rel > 0, half, 0)
    n = jnp.abs(rel)
    nf = jnp.maximum(n, 1).astype(jnp.float32)
    large = max_exact + (jnp.log(nf / max_exact) / math.log(MAX_DISTANCE / max_exact)
                         * (half - max_exact)).astype(jnp.int32)
    large = jnp.minimum(large, half - 1)
    return ret + jnp.where(n < max_exact, n, large)


def _bias_tiles(table, t):
    tab = table.astype(jnp.float32)

    def lookup(rel):
        bucket = _t5_bucket(rel)
        out = jnp.zeros((tab.shape[1],) + rel.shape, jnp.float32)
        for b in range(N_BUCKETS):
            out = jnp.where(bucket == b, tab[b].reshape((-1,) + (1,) * rel.ndim), out)
        return out

    far = lookup(jnp.full((1, 1), -MAX_DISTANCE, jnp.int32))
    k = jnp.arange(t)[:, None]
    q = jnp.arange(t)[None, :]
    rel = k - q
    diag = (lookup(rel) - far) * LOG2E
    diag = jnp.where((k // CHUNK) <= (q // CHUNK), diag, NEG_BIG)
    rel_sub = rel[t - LANES:] - t
    sub = (lookup(rel_sub) - far) * LOG2E
    return diag, sub


def _split_bf16(x, parts):
    out = []
    for _ in range(parts - 1):
        bits = lax.bitcast_convert_type(x, jnp.uint32) & jnp.uint32(0xFFFF0000)
        piece = lax.bitcast_convert_type(bits, jnp.float32)
        out.append(piece.astype(jnp.bfloat16))
        x = x - piece
    out.append(x.astype(jnp.bfloat16))
    return out


def _forward(x, w_in, b_f, lambda_q1, lambda_k1, lambda_q2, lambda_k2, subln_g, w_out,
             norm_attn_g, norm_mlp_g, w_mlp_in, w_mlp_out, rel_bias_table, final_norm_g,
             *, attn_tile, row_tile):
    batch, seq, _ = x.shape
    depth = w_in.shape[0]
    n = batch * seq
    t = attn_tile
    tm = min(row_tile, n)
    qp, dp = FOX_Q_PARTS, DIFF_Q_PARTS
    assert seq % (qp * t) == 0 and seq % (dp * t) == 0 and n % tm == 0
    assert t % LANES == 0 and t > LANES
    pairs = [(i, j) for i in range(seq // (dp * t)) for j in range(dp * (i + 1))]
    pairs_i = jnp.asarray([p[0] for p in pairs], jnp.int32)
    pairs_j = jnp.asarray([p[1] for p in pairs], jnp.int32)
    fox_pairs = [(i, j) for i in range(seq // (qp * t))
                 for j in range(qp * (i + 1) - 1, -1, -1)]
    fox_i = jnp.asarray([p[0] for p in fox_pairs], jnp.int32)
    fox_j = jnp.asarray([p[1] for p in fox_pairs], jnp.int32)
    bias_diag, bias_sub = _bias_tiles(rel_bias_table, t)

    o_fq, o_fk, o_fv = 0, FOX_WIDTH, 2 * FOX_WIDTH
    o_ff = 3 * FOX_WIDTH
    o_dq = o_ff + FOX_HEADS
    o_dk = o_dq + DIFF_WIDTH
    o_dv = o_dk + DIFF_WIDTH
    qscale = LOG2E / math.sqrt(HEAD_DIM)

    x2d = x.reshape(n, D_MODEL)
    for l in range(depth):
        w = w_in[l]
        wqv_t = jnp.concatenate([
            w[:, o_fq:o_fq + FOX_WIDTH] * qscale, w[:, o_fv:o_fv + FOX_WIDTH],
            w[:, o_dq:o_dq + DIFF_WIDTH] * qscale, w[:, o_dv:o_dv + DIFF_WIDTH]],
            axis=1).T.astype(jnp.bfloat16)
        wk = jnp.concatenate([w[:, o_fk:o_fk + FOX_WIDTH], w[:, o_dk:o_dk + DIFF_WIDTH]],
                             axis=1).astype(jnp.bfloat16)
        wf = jnp.pad(w[:, o_ff:o_ff + FOX_HEADS], ((0, 0), (0, LANES - FOX_HEADS))
                     ).astype(jnp.bfloat16)
        bf = jnp.pad(b_f[l].astype(jnp.float32), (0, LANES - FOX_HEADS)).reshape(1, LANES)

        zt, zk, lf, stats = _in_proj(x2d, norm_attn_g[l].reshape(1, D_MODEL), wqv_t, wk, wf, bf, t)

        lf_rows = lf[:, :FOX_HEADS].reshape(batch, seq, FOX_HEADS).transpose(0, 2, 1)
        c = _decay_scan(lf_rows.reshape(batch * FOX_HEADS * seq // LANES, LANES), seq // LANES)
        c = c.reshape(batch, FOX_HEADS, seq)
        parts = _split_bf16(c * (-LOG2E), N_DECAY_PARTS)
        dec = jnp.stack(parts, axis=2).transpose(0, 3, 1, 2)
        dec = jnp.pad(dec.reshape(batch, seq, FOX_HEADS * N_DECAY_PARTS),
                      ((0, 0), (0, 0), (0, LANES - FOX_HEADS * N_DECAY_PARTS)))

        first_tile = _fox_first_tiles(stats, parts, fox_i, batch, seq, t, qp)
        fox_t = _fox_attn(zt, zk, dec, fox_i, fox_j, first_tile, batch, seq, t, qp)

        lambda_init = 0.8 - 0.6 * math.exp(-0.3 * l)
        lam_rows = jnp.pad(
            jnp.stack([lambda_q1[l], lambda_k1[l], lambda_q2[l], lambda_k2[l]]).astype(jnp.float32),
            ((0, 4), (0, LANES - HEAD_DIM)))
        dif_t = _diff_attn(zt, zk, bias_diag, bias_sub, lam_rows,
                           subln_g[l].astype(jnp.float32).reshape(DIFF_V_DIM, 1),
                           pairs_i, pairs_j, batch, seq, t, dp, lambda_init)

        x2d = _out_mlp(x2d, fox_t, dif_t, w_out[l].astype(jnp.bfloat16),
                       norm_mlp_g[l].reshape(1, D_MODEL), w_mlp_in[l].astype(jnp.bfloat16),
                       w_mlp_out[l].astype(jnp.bfloat16), final_norm_g.reshape(1, D_MODEL),
                       tm, final_norm=(l == depth - 1))
    return x2d.reshape(batch, seq, D_MODEL)


def kernel(x, w_in, b_f, lambda_q1, lambda_k1, lambda_q2, lambda_k2, subln_g, w_out,
           norm_attn_g, norm_mlp_g, w_mlp_in, w_mlp_out, rel_bias_table, final_norm_g):
    return _forward(x, w_in, b_f, lambda_q1, lambda_k1, lambda_q2, lambda_k2, subln_g, w_out,
                    norm_attn_g, norm_mlp_g, w_mlp_in, w_mlp_out, rel_bias_table, final_norm_g,
                    attn_tile=ATTN_TILE, row_tile=ROW_TILE)
```

```python
import functools
import math

import jax
import jax.numpy as jnp
import numpy as np
from jax import lax
from jax.experimental import pallas as pl
from jax.experimental.pallas import tpu as pltpu

D_MODEL = 1024
CHUNK = 64
FOX_HEADS = 8
HEAD_DIM = 64
FOX_WIDTH = FOX_HEADS * HEAD_DIM
DIFF_HEADS = 4
DIFF_V_DIM = 2 * HEAD_DIM
DIFF_WIDTH = DIFF_HEADS * DIFF_V_DIM
D_FF = 4 * D_MODEL
N_BUCKETS = 32
MAX_DISTANCE = 128
NORM_EPS = 1e-5
SUBLN_EPS = 1e-5

LANES = 128
LOG2E = math.log2(math.e)
NEG_BIG = -1e30
N_DECAY_PARTS = 3
SUM_ROWS = 16
FOX_Q_PARTS = 2
DIFF_Q_PARTS = 2
DEAD_GAP = 152.0
NORM_SLACK = 1.001
ATTN_TILE = 512
ROW_TILE = 512
FF_CHUNK = 1024
VMEM_LIMIT = 48 * 1024 * 1024

_NT = (((1,), (1,)), ((), ()))
_TN = (((0,), (0,)), ((), ()))


def _rms(x, g, eps):
    return x * lax.rsqrt(jnp.mean(x * x, axis=-1, keepdims=True) + eps) * g


def _in_proj_kernel(x_ref, g_ref, wqv_ref, wk_ref, wf_ref, bf_ref, ind_ref,
                    zt_ref, zk_ref, lf_ref, st_ref):
    h = _rms(x_ref[...], g_ref[...], NORM_EPS).astype(jnp.bfloat16)
    zt = lax.dot_general(wqv_ref[...], h, _NT, preferred_element_type=jnp.float32)
    zt = zt.astype(jnp.bfloat16)
    zt_ref[...] = zt
    zk = jnp.dot(h, wk_ref[...], preferred_element_type=jnp.float32).astype(jnp.bfloat16)
    zk_ref[...] = zk
    f = jnp.dot(h, wf_ref[...], preferred_element_type=jnp.float32) + bf_ref[...]
    lf_ref[...] = jnp.minimum(f, 0.0) - jnp.log1p(jnp.exp(-jnp.abs(f)))
    tm = zt.shape[1]
    q = zt[:FOX_WIDTH].astype(jnp.float32)
    qn = jnp.sum((q * q).reshape(FOX_HEADS, HEAD_DIM, tm), axis=1)
    k = zk[:, :FOX_WIDTH].astype(jnp.float32)
    k2 = (k * k * (1.0 + 2.0 ** -7)).astype(jnp.bfloat16)
    kn = jnp.dot(k2, ind_ref[...], preferred_element_type=jnp.float32)
    st_ref[0, :FOX_HEADS, :] = jnp.broadcast_to(jnp.max(qn, axis=1, keepdims=True),
                                                (FOX_HEADS, LANES))
    st_ref[0, FOX_HEADS:, :] = jnp.broadcast_to(jnp.max(kn, axis=0, keepdims=True),
                                                (FOX_HEADS, LANES))


def _in_proj(x2d, g, wqv_t, wk, wf, bf, tm):
    n = x2d.shape[0]
    nqv = wqv_t.shape[0]
    nk = wk.shape[1]
    const = lambda i: (0, 0)
    head_of_col = jnp.arange(FOX_WIDTH)[:, None] // HEAD_DIM == jnp.arange(LANES)[None, :]
    return pl.pallas_call(
        _in_proj_kernel,
        grid=(n // tm,),
        in_specs=[
            pl.BlockSpec((tm, D_MODEL), lambda i: (i, 0)),
            pl.BlockSpec((1, D_MODEL), const),
            pl.BlockSpec((nqv, D_MODEL), const),
            pl.BlockSpec((D_MODEL, nk), const),
            pl.BlockSpec((D_MODEL, LANES), const),
            pl.BlockSpec((1, LANES), const),
            pl.BlockSpec((FOX_WIDTH, LANES), const),
        ],
        out_specs=[
            pl.BlockSpec((nqv, tm), lambda i: (0, i)),
            pl.BlockSpec((tm, nk), lambda i: (i, 0)),
            pl.BlockSpec((tm, LANES), lambda i: (i, 0)),
            pl.BlockSpec((1, 2 * FOX_HEADS, LANES), lambda i: (i, 0, 0)),
        ],
        out_shape=[
            jax.ShapeDtypeStruct((nqv, n), jnp.bfloat16),
            jax.ShapeDtypeStruct((n, nk), jnp.bfloat16),
            jax.ShapeDtypeStruct((n, LANES), jnp.float32),
            jax.ShapeDtypeStruct((n // tm, 2 * FOX_HEADS, LANES), jnp.float32),
        ],
        compiler_params=pltpu.CompilerParams(
            dimension_semantics=("arbitrary",), vmem_limit_bytes=VMEM_LIMIT),
        name="in_proj",
    )(x2d, g, wqv_t, wk, wf, bf, head_of_col.astype(jnp.bfloat16))


def _decay_scan_kernel(x_ref, o_ref, *, rows_per_seq):
    x = x_ref[...]
    lane = lax.broadcasted_iota(jnp.int32, x.shape, 1)
    d = 1
    while d < LANES:
        x = x + jnp.where(lane >= d, pltpu.roll(x, d, 1), 0.0)
        d *= 2
    tot = jnp.broadcast_to(x[:, LANES - 1:LANES], x.shape)
    row = lax.broadcasted_iota(jnp.int32, x.shape, 0) % rows_per_seq
    inc = tot
    d = 1
    while d < rows_per_seq:
        inc = inc + jnp.where(row >= d, pltpu.roll(inc, d, 0), 0.0)
        d *= 2
    o_ref[...] = x + (inc - tot)


def _decay_scan(lf_rows, rows_per_seq):
    return pl.pallas_call(
        functools.partial(_decay_scan_kernel, rows_per_seq=rows_per_seq),
        out_shape=jax.ShapeDtypeStruct(lf_rows.shape, jnp.float32),
        compiler_params=pltpu.CompilerParams(vmem_limit_bytes=VMEM_LIMIT),
        name="decay_scan",
    )(lf_rows)


def _ones_rows(t):
    return jnp.where(lax.broadcasted_iota(jnp.int32, (SUM_ROWS, t), 0) == 0, 1.0, 0.0
                     ).astype(jnp.bfloat16)


def _flash_units(n_groups, t, kinds, score_fn, value_fn, s_bufs, m_ref, acc_ref):
    units = [(g, slice(c * t, (c + 1) * t), kind) for g in range(n_groups)
             for c, kind in enumerate(kinds) if kind is not None]

    def produce(n):
        g, cols, kind = units[n]
        s = score_fn(g, cols, kind)
        s_bufs[n % 2][...] = s
        return jnp.max(s, axis=0, keepdims=True)

    mx = produce(0)
    for n, (g, cols, _) in enumerate(units):
        mx_next = produce(n + 1) if n + 1 < len(units) else None
        m_prev = m_ref[g, :, cols]
        m_new = jnp.maximum(m_prev, mx)
        m_ref[g, :, cols] = m_new
        p = jnp.exp2(s_bufs[n % 2][...] - m_new).astype(jnp.bfloat16)
        pv = jnp.dot(value_fn(g), p, preferred_element_type=jnp.float32)
        acc_ref[g, :, cols] = jnp.exp2(m_prev - m_new) * acc_ref[g, :, cols] + pv
        mx = mx_next


def _init_stats(m_ref, acc_ref):
    m_ref[...] = jnp.full(m_ref.shape, NEG_BIG, jnp.float32)
    acc_ref[...] = jnp.zeros(acc_ref.shape, jnp.float32)


def _fox_kernel(i_ref, j_ref, j0_ref, qt_ref, vt_ref, k_ref, dec_ref, o_ref,
                s0_ref, s1_ref, m_ref, acc_ref):
    p_id = pl.program_id(1)
    t = s0_ref.shape[0]
    parts = qt_ref.shape[1] // t
    rel = j_ref[p_id] - parts * i_ref[p_id]
    j = j_ref[p_id]
    step = pl.program_id(0) * pl.num_programs(1) + p_id
    n_steps = pl.num_programs(0) * pl.num_programs(1)
    first = [j0_ref[c * n_steps + step] for c in range(parts)]

    @pl.when(rel == parts - 1)
    def _():
        _init_stats(m_ref, acc_ref)

    ones = _ones_rows(t)
    zeros = jnp.zeros((HEAD_DIM, t), jnp.bfloat16)
    dec_row = lax.broadcasted_iota(jnp.int32, (LANES, t), 0)

    def rows(h):
        return slice(h * HEAD_DIM, (h + 1) * HEAD_DIM)

    def scores(h, cols, kind):
        q = qt_ref[rows(h), cols]
        pick = ((dec_row >= N_DECAY_PARTS * h) & (dec_row < N_DECAY_PARTS * (h + 1)))
        qa = jnp.concatenate(([q, zeros] if h % 2 == 0 else [zeros, q])
                             + [jnp.where(pick, 1.0, 0.0).astype(jnp.bfloat16)], axis=0)
        pair = slice((h // 2) * LANES, (h // 2 + 1) * LANES)
        ka = jnp.concatenate([k_ref[:, pair], dec_ref[0]], axis=1)
        s = jnp.dot(ka, qa, preferred_element_type=jnp.float32)
        if kind == "diag":
            kpos = lax.broadcasted_iota(jnp.int32, s.shape, 0)
            qpos = lax.broadcasted_iota(jnp.int32, s.shape, 1)
            s = jnp.where(kpos <= qpos, s, NEG_BIG)
        return s

    def values(h):
        return jnp.concatenate([vt_ref[rows(h), :], ones], axis=0)

    def sweep(kinds):
        _flash_units(FOX_HEADS, t, kinds, scores, values, (s0_ref, s1_ref), m_ref, acc_ref)

    for r in range(parts):
        @pl.when(rel == r)
        def _(r=r):
            sweep([None] * r + ["diag"] + ["past"] * (parts - 1 - r))

    for a in range(1, parts + 1):
        cond = (rel < 0) & (j >= first[a - 1])
        if a < parts:
            cond = cond & (j < first[a])

        @pl.when(cond)
        def _(a=a):
            sweep(["past"] * a + [None] * (parts - a))

    @pl.when(j == 0)
    def _():
        for h in range(FOX_HEADS):
            a = acc_ref[h]
            o_ref[rows(h), :] = (a[:HEAD_DIM] / a[HEAD_DIM:HEAD_DIM + 1]).astype(o_ref.dtype)


def _fox_attn(zt, zk, dec, pairs_i, pairs_j, first_tile, batch, seq, t, parts):
    nt = seq // t
    tq = parts * t
    nq = seq // tq
    n = batch * seq
    n_pairs = pairs_i.shape[0]

    def key_tile(b, p, jj, j0):
        return jnp.maximum(jj[p], j0[b * n_pairs + p])

    grid_spec = pltpu.PrefetchScalarGridSpec(
        num_scalar_prefetch=3,
        grid=(batch, n_pairs),
        in_specs=[
            pl.BlockSpec((FOX_WIDTH, tq), lambda b, p, ii, jj, j0: (0, b * nq + ii[p])),
            pl.BlockSpec((FOX_WIDTH, t),
                         lambda b, p, ii, jj, j0: (1, b * nt + key_tile(b, p, jj, j0))),
            pl.BlockSpec((t, FOX_WIDTH),
                         lambda b, p, ii, jj, j0: (b * nt + key_tile(b, p, jj, j0), 0)),
            pl.BlockSpec((1, t, LANES), lambda b, p, ii, jj, j0: (b, key_tile(b, p, jj, j0), 0)),
        ],
        out_specs=pl.BlockSpec((FOX_WIDTH, tq), lambda b, p, ii, jj, j0: (0, b * nq + ii[p])),
        scratch_shapes=[
            pltpu.VMEM((t, t), jnp.float32),
            pltpu.VMEM((t, t), jnp.float32),
            pltpu.VMEM((FOX_HEADS, 1, tq), jnp.float32),
            pltpu.VMEM((FOX_HEADS, HEAD_DIM + SUM_ROWS, tq), jnp.float32),
        ],
    )
    return pl.pallas_call(
        _fox_kernel,
        grid_spec=grid_spec,
        out_shape=jax.ShapeDtypeStruct((FOX_WIDTH, n), jnp.bfloat16),
        compiler_params=pltpu.CompilerParams(
            dimension_semantics=("arbitrary", "arbitrary"), vmem_limit_bytes=VMEM_LIMIT),
        name="fox_attn",
    )(pairs_i, pairs_j, first_tile, zt, zt, zk, dec)


def _fox_first_tiles(stats, pieces, pairs_i, batch, seq, t, parts):
    nt = seq // t
    nq = nt // parts
    f32 = jnp.float32
    aq = jnp.sqrt(stats[:, :FOX_HEADS, 0].reshape(batch, nt, FOX_HEADS)) * NORM_SLACK
    ak = jnp.sqrt(stats[:, FOX_HEADS, :FOX_HEADS].reshape(batch, nt, FOX_HEADS)) * NORM_SLACK
    d = sum(p.astype(f32) for p in pieces).reshape(batch, FOX_HEADS, nt, t)
    d_first = d[..., 0].transpose(0, 2, 1)
    d_last = d[..., t - 1].transpose(0, 2, 1)
    row_floor = d_first - aq * ak
    tile_top = aq[:, :, None, :] * ak[:, None, :, :] + d_last[:, None, :, :]
    alive = jnp.any(tile_top >= row_floor[:, :, None, :] - DEAD_GAP, axis=-1)
    tile = jnp.arange(nt)
    alive = alive | (tile[None, None, :] >= tile[None, :, None])
    first = jnp.argmax(alive, axis=-1).astype(jnp.int32).reshape(batch, nq, parts)
    first = lax.cummin(first, axis=2, reverse=True)
    return first[:, pairs_i].transpose(2, 0, 1).reshape(-1)


def _diff_kernel(i_ref, j_ref, qt_ref, vt_ref, kd_ref, bd_ref, bs_ref, lam_ref, g_ref,
                 o_ref, s0_ref, s1_ref, m_ref, acc_ref, *, lambda_init):
    p_id = pl.program_id(1)
    t = s0_ref.shape[0]
    parts = qt_ref.shape[1] // t
    j = j_ref[p_id]
    rel = j - parts * i_ref[p_id]

    @pl.when(j == 0)
    def _():
        _init_stats(m_ref, acc_ref)

    first_map = lax.broadcasted_iota(jnp.int32, (DIFF_V_DIM, t), 0) < HEAD_DIM
    ones = _ones_rows(t)

    def rows(h):
        return slice(h * DIFF_V_DIM, (h + 1) * DIFF_V_DIM)

    def scores(u, cols, kind):
        h, mp = divmod(u, 2)
        q = qt_ref[rows(h), cols]
        zero = jnp.zeros_like(q)
        qm = jnp.where(first_map, q, zero) if mp == 0 else jnp.where(first_map, zero, q)
        kd = kd_ref[:, rows(h)]
        if kind == "sub":
            s_far = jnp.dot(kd[:t - LANES], qm, preferred_element_type=jnp.float32)
            s_near = jnp.dot(kd[t - LANES:], qm, preferred_element_type=jnp.float32)
            return jnp.concatenate([s_far, s_near + bs_ref[h]], axis=0)
        s = jnp.dot(kd, qm, preferred_element_type=jnp.float32)
        if kind == "diag":
            s = s + bd_ref[h]
        return s

    def values(u):
        return jnp.concatenate([vt_ref[rows(u // 2), :], ones], axis=0)

    def kind_of(d):
        return None if d > 0 else "diag" if d == 0 else "sub" if d == -1 else "far"

    def sweep(r):
        kinds = [kind_of(r - c) for c in range(parts)]
        _flash_units(2 * DIFF_HEADS, t, kinds, scores, values, (s0_ref, s1_ref), m_ref, acc_ref)

    @pl.when(rel < -1)
    def _():
        sweep(-2)

    for r in range(-1, parts):
        @pl.when(rel == r)
        def _(r=r):
            sweep(r)

    @pl.when(rel == parts - 1)
    def _():
        lam_rows = lam_ref[...]
        dot1 = jnp.sum(lam_rows[0:1] * lam_rows[1:2], axis=-1, keepdims=True)
        dot2 = jnp.sum(lam_rows[2:3] * lam_rows[3:4], axis=-1, keepdims=True)
        lam = jnp.exp(dot1) - jnp.exp(dot2) + lambda_init
        g = g_ref[...] * (1.0 - lambda_init)
        for h in range(DIFF_HEADS):
            a1 = acc_ref[2 * h]
            a2 = acc_ref[2 * h + 1]
            d = (a1[:DIFF_V_DIM] / a1[DIFF_V_DIM:DIFF_V_DIM + 1]
                 - lam * (a2[:DIFF_V_DIM] / a2[DIFF_V_DIM:DIFF_V_DIM + 1]))
            y = d * lax.rsqrt(jnp.mean(d * d, axis=0, keepdims=True) + SUBLN_EPS)
            o_ref[rows(h), :] = (y * g).astype(o_ref.dtype)


def _diff_attn(zt, kd, bias_diag, bias_sub, lam_rows, g_col, pairs_i, pairs_j,
               batch, seq, t, parts, lambda_init):
    nt = seq // t
    tq = parts * t
    nq = seq // tq
    n = batch * seq
    grid_spec = pltpu.PrefetchScalarGridSpec(
        num_scalar_prefetch=2,
        grid=(batch, pairs_i.shape[0]),
        in_specs=[
            pl.BlockSpec((DIFF_WIDTH, tq), lambda b, p, ii, jj: (2, b * nq + ii[p])),
            pl.BlockSpec((DIFF_WIDTH, t), lambda b, p, ii, jj: (3, b * nt + jj[p])),
            pl.BlockSpec((t, DIFF_WIDTH), lambda b, p, ii, jj: (b * nt + jj[p], 1)),
            pl.BlockSpec((DIFF_HEADS, t, t), lambda b, p, ii, jj: (0, 0, 0)),
            pl.BlockSpec((DIFF_HEADS, LANES, t), lambda b, p, ii, jj: (0, 0, 0)),
            pl.BlockSpec((8, LANES), lambda b, p, ii, jj: (0, 0)),
            pl.BlockSpec((DIFF_V_DIM, 1), lambda b, p, ii, jj: (0, 0)),
        ],
        out_specs=pl.BlockSpec((DIFF_WIDTH, tq), lambda b, p, ii, jj: (0, b * nq + ii[p])),
        scratch_shapes=[
            pltpu.VMEM((t, t), jnp.float32),
            pltpu.VMEM((t, t), jnp.float32),
            pltpu.VMEM((2 * DIFF_HEADS, 1, tq), jnp.float32),
            pltpu.VMEM((2 * DIFF_HEADS, DIFF_V_DIM + SUM_ROWS, tq), jnp.float32),
        ],
    )
    return pl.pallas_call(
        functools.partial(_diff_kernel, lambda_init=lambda_init),
        grid_spec=grid_spec,
        out_shape=jax.ShapeDtypeStruct((DIFF_WIDTH, n), jnp.bfloat16),
        compiler_params=pltpu.CompilerParams(
            dimension_semantics=("arbitrary", "arbitrary"), vmem_limit_bytes=VMEM_LIMIT),
        name="diff_attn",
    )(pairs_i, pairs_j, zt, zt, kd, bias_diag, bias_sub, lam_rows, g_col)


def _out_mlp_kernel(x_ref, fox_ref, dif_ref, wo_ref, g_ref, w1_ref, w2_ref, gf_ref, o_ref,
                    *, final_norm):
    y = lax.dot_general(fox_ref[...], wo_ref[:FOX_WIDTH, :], _TN,
                        preferred_element_type=jnp.float32)
    y = y + lax.dot_general(dif_ref[...], wo_ref[FOX_WIDTH:, :], _TN,
                            preferred_element_type=jnp.float32)
    x1 = x_ref[...] + y
    h = _rms(x1, g_ref[...], NORM_EPS).astype(jnp.bfloat16)
    acc = x1
    for c in range(D_FF // FF_CHUNK):
        u = jnp.dot(h, w1_ref[:, c * FF_CHUNK:(c + 1) * FF_CHUNK],
                    preferred_element_type=jnp.float32)
        u = jnp.square(jnp.maximum(u, 0.0)).astype(jnp.bfloat16)
        acc = acc + jnp.dot(u, w2_ref[c * FF_CHUNK:(c + 1) * FF_CHUNK, :],
                            preferred_element_type=jnp.float32)
    if final_norm:
        acc = _rms(acc, gf_ref[...], NORM_EPS)
    o_ref[...] = acc


def _out_mlp(x2d, fox_t, dif_t, wo, g, w1, w2, gf, tm, final_norm):
    n = x2d.shape[0]
    const = lambda i: (0, 0)
    return pl.pallas_call(
        functools.partial(_out_mlp_kernel, final_norm=final_norm),
        grid=(n // tm,),
        in_specs=[
            pl.BlockSpec((tm, D_MODEL), lambda i: (i, 0)),
            pl.BlockSpec((FOX_WIDTH, tm), lambda i: (0, i)),
            pl.BlockSpec((DIFF_WIDTH, tm), lambda i: (0, i)),
            pl.BlockSpec((D_MODEL, D_MODEL), const, pipeline_mode=pl.Buffered(1)),
            pl.BlockSpec((1, D_MODEL), const),
            pl.BlockSpec((D_MODEL, D_FF), const, pipeline_mode=pl.Buffered(1)),
            pl.BlockSpec((D_FF, D_MODEL), const, pipeline_mode=pl.Buffered(1)),
            pl.BlockSpec((1, D_MODEL), const),
        ],
        out_specs=pl.BlockSpec((tm, D_MODEL), lambda i: (i, 0)),
        out_shape=jax.ShapeDtypeStruct((n, D_MODEL), jnp.float32),
        compiler_params=pltpu.CompilerParams(
            dimension_semantics=("arbitrary",), vmem_limit_bytes=VMEM_LIMIT),
        name="out_mlp",
    )(x2d, fox_t, dif_t, wo, g, w1, w2, gf)


def _t5_bucket(rel):
    half = N_BUCKETS // 2
    max_exact = half // 2
    ret = jnp.where(rel > 0, half, 0)
    n = jnp.abs(rel)
    nf = jnp.maximum(n, 1).astype(jnp.float32)
    large = max_exact + (jnp.log(nf / max_exact) / math.log(MAX_DISTANCE / max_exact)
                         * (half - max_exact)).astype(jnp.int32)
    large = jnp.minimum(large, half - 1)
    return ret + jnp.where(n < max_exact, n, large)


def _bias_tiles(table, t):
    tab = table.astype(jnp.float32)

    def lookup(rel):
        bucket = _t5_bucket(rel)
        out = jnp.zeros((tab.shape[1],) + rel.shape, jnp.float32)
        for b in range(N_BUCKETS):
            out = jnp.where(bucket == b, tab[b].reshape((-1,) + (1,) * rel.ndim), out)
        return out

    far = lookup(jnp.full((1, 1), -MAX_DISTANCE, jnp.int32))
    k = jnp.arange(t)[:, None]
    q = jnp.arange(t)[None, :]
    rel = k - q
    diag = (lookup(rel) - far) * LOG2E
    diag = jnp.where((k // CHUNK) <= (q // CHUNK), diag, NEG_BIG)
    rel_sub = rel[t - LANES:] - t
    sub = (lookup(rel_sub) - far) * LOG2E
    return diag, sub


def _split_bf16(x, parts):
    out = []
    for _ in range(parts - 1):
        bits = lax.bitcast_convert_type(x, jnp.uint32) & jnp.uint32(0xFFFF0000)
        piece = lax.bitcast_convert_type(bits, jnp.float32)
        out.append(piece.astype(jnp.bfloat16))
        x = x - piece
    out.append(x.astype(jnp.bfloat16))
    return out


def _forward(x, w_in, b_f, lambda_q1, lambda_k1, lambda_q2, lambda_k2, subln_g, w_out,
             norm_attn_g, norm_mlp_g, w_mlp_in, w_mlp_out, rel_bias_table, final_norm_g,
             *, attn_tile, row_tile):
    batch, seq, _ = x.shape
    depth = w_in.shape[0]
    n = batch * seq
    t = attn_tile
    tm = min(row_tile, n)
    qp, dp = FOX_Q_PARTS, DIFF_Q_PARTS
    assert seq % (qp * t) == 0 and seq % (dp * t) == 0 and n % tm == 0
    assert t % LANES == 0 and t > LANES
    pairs = [(i, j) for i in range(seq // (dp * t)) for j in range(dp * (i + 1))]
    pairs_i = jnp.asarray([p[0] for p in pairs], jnp.int32)
    pairs_j = jnp.asarray([p[1] for p in pairs], jnp.int32)
    fox_pairs = [(i, j) for i in range(seq // (qp * t))
                 for j in range(qp * (i + 1) - 1, -1, -1)]
    fox_i = jnp.asarray([p[0] for p in fox_pairs], jnp.int32)
    fox_j = jnp.asarray([p[1] for p in fox_pairs], jnp.int32)
    bias_diag, bias_sub = _bias_tiles(rel_bias_table, t)

    o_fq, o_fk, o_fv = 0, FOX_WIDTH, 2 * FOX_WIDTH
    o_ff = 3 * FOX_WIDTH
    o_dq = o_ff + FOX_HEADS
    o_dk = o_dq + DIFF_WIDTH
    o_dv = o_dk + DIFF_WIDTH
    qscale = LOG2E / math.sqrt(HEAD_DIM)

    x2d = x.reshape(n, D_MODEL)
    for l in range(depth):
        w = w_in[l]
        wqv_t = jnp.concatenate([
            w[:, o_fq:o_fq + FOX_WIDTH] * qscale, w[:, o_fv:o_fv + FOX_WIDTH],
            w[:, o_dq:o_dq + DIFF_WIDTH] * qscale, w[:, o_dv:o_dv + DIFF_WIDTH]],
            axis=1).T.astype(jnp.bfloat16)
        wk = jnp.concatenate([w[:, o_fk:o_fk + FOX_WIDTH], w[:, o_dk:o_dk + DIFF_WIDTH]],
                             axis=1).astype(jnp.bfloat16)
        wf = jnp.pad(w[:, o_ff:o_ff + FOX_HEADS], ((0, 0), (0, LANES - FOX_HEADS))
                     ).astype(jnp.bfloat16)
        bf = jnp.pad(b_f[l].astype(jnp.float32), (0, LANES - FOX_HEADS)).reshape(1, LANES)

        zt, zk, lf, stats = _in_proj(x2d, norm_attn_g[l].reshape(1, D_MODEL), wqv_t, wk, wf, bf, t)

        lf_rows = lf[:, :FOX_HEADS].reshape(batch, seq, FOX_HEADS).transpose(0, 2, 1)
        c = _decay_scan(lf_rows.reshape(batch * FOX_HEADS * seq // LANES, LANES), seq // LANES)
        c = c.reshape(batch, FOX_HEADS, seq)
        parts = _split_bf16(c * (-LOG2E), N_DECAY_PARTS)
        dec = jnp.stack(parts, axis=2).transpose(0, 3, 1, 2)
        dec = jnp.pad(dec.reshape(batch, seq, FOX_HEADS * N_DECAY_PARTS),
                      ((0, 0), (0, 0), (0, LANES - FOX_HEADS * N_DECAY_PARTS)))

        first_tile = _fox_first_tiles(stats, parts, fox_i, batch, seq, t, qp)
        fox_t = _fox_attn(zt, zk, dec, fox_i, fox_j, first_tile, batch, seq, t, qp)

        lambda_init = 0.8 - 0.6 * math.exp(-0.3 * l)
        lam_rows = jnp.pad(
            jnp.stack([lambda_q1[l], lambda_k1[l], lambda_q2[l], lambda_k2[l]]).astype(jnp.float32),
            ((0, 4), (0, LANES - HEAD_DIM)))
        dif_t = _diff_attn(zt, zk, bias_diag, bias_sub, lam_rows,
                           subln_g[l].astype(jnp.float32).reshape(DIFF_V_DIM, 1),
                           pairs_i, pairs_j, batch, seq, t, dp, lambda_init)

        x2d = _out_mlp(x2d, fox_t, dif_t, w_out[l].astype(jnp.bfloat16),
                       norm_mlp_g[l].reshape(1, D_MODEL), w_mlp_in[l].astype(jnp.bfloat16),
                       w_mlp_out[l].astype(jnp.bfloat16), final_norm_g.reshape(1, D_MODEL),
                       tm, final_norm=(l == depth - 1))
    return x2d.reshape(batch, seq, D_MODEL)


def kernel(x, w_in, b_f, lambda_q1, lambda_k1, lambda_q2, lambda_k2, subln_g, w_out,
           norm_attn_g, norm_mlp_g, w_mlp_in, w_mlp_out, rel_bias_table, final_norm_g):
    return _forward(x, w_in, b_f, lambda_q1, lambda_k1, lambda_q2, lambda_k2, subln_g, w_out,
                    norm_attn_g, norm_mlp_g, w_mlp_in, w_mlp_out, rel_bias_table, final_norm_g,
                    attn_tile=ATTN_TILE, row_tile=ROW_TILE)
```

```python
import functools
import math

import jax
import jax.numpy as jnp
from jax import lax
from jax.experimental import pallas as pl
from jax.experimental.pallas import tpu as pltpu

D_MODEL = 1024
CHUNK = 64
FOX_HEADS = 8
HEAD_DIM = 64
FOX_WIDTH = FOX_HEADS * HEAD_DIM
DIFF_HEADS = 4
DIFF_V_DIM = 2 * HEAD_DIM
DIFF_WIDTH = DIFF_HEADS * DIFF_V_DIM
D_FF = 4 * D_MODEL
N_BUCKETS = 32
MAX_DISTANCE = 128
NORM_EPS = 1e-5
SUBLN_EPS = 1e-5

LANES = 128
LOG2E = math.log2(math.e)
NEG_BIG = -1e30
N_DECAY_PARTS = 3
SUM_ROWS = 16
FOX_Q_PARTS = 2
DIFF_Q_PARTS = 2
SCORE_BUFS = 3
DEAD_GAP = 152.0
NORM_SLACK = 1.001
ATTN_TILE = 512
ROW_TILE = 512
FF_CHUNK = 1024
VMEM_LIMIT = 48 * 1024 * 1024

_NT = (((1,), (1,)), ((), ()))
_TN = (((0,), (0,)), ((), ()))


def _rms(x, g, eps):
    return x * lax.rsqrt(jnp.mean(x * x, axis=-1, keepdims=True) + eps) * g


def _in_proj_kernel(x_ref, g_ref, wqv_ref, wk_ref, wf_ref, bf_ref, ind_ref,
                    zt_ref, zk_ref, lf_ref, st_ref):
    h = _rms(x_ref[...], g_ref[...], NORM_EPS).astype(jnp.bfloat16)
    zt = lax.dot_general(wqv_ref[...], h, _NT, preferred_element_type=jnp.float32)
    zt = zt.astype(jnp.bfloat16)
    zt_ref[...] = zt
    zk = jnp.dot(h, wk_ref[...], preferred_element_type=jnp.float32).astype(jnp.bfloat16)
    zk_ref[...] = zk
    f = jnp.dot(h, wf_ref[...], preferred_element_type=jnp.float32) + bf_ref[...]
    lf_ref[...] = jnp.minimum(f, 0.0) - jnp.log1p(jnp.exp(-jnp.abs(f)))
    tm = zt.shape[1]
    q = zt[:FOX_WIDTH].astype(jnp.float32)
    qn = jnp.sum((q * q).reshape(FOX_HEADS, HEAD_DIM, tm), axis=1)
    k = zk[:, :FOX_WIDTH].astype(jnp.float32)
    k2 = (k * k * (1.0 + 2.0 ** -7)).astype(jnp.bfloat16)
    kn = jnp.dot(k2, ind_ref[...], preferred_element_type=jnp.float32)
    st_ref[0, :FOX_HEADS, :] = jnp.broadcast_to(jnp.max(qn, axis=1, keepdims=True),
                                                (FOX_HEADS, LANES))
    st_ref[0, FOX_HEADS:, :] = jnp.broadcast_to(jnp.max(kn, axis=0, keepdims=True),
                                                (FOX_HEADS, LANES))


def _in_proj(x2d, g, wqv_t, wk, wf, bf, tm):
    n = x2d.shape[0]
    nqv = wqv_t.shape[0]
    nk = wk.shape[1]
    const = lambda i: (0, 0)
    head_of_col = jnp.arange(FOX_WIDTH)[:, None] // HEAD_DIM == jnp.arange(LANES)[None, :]
    return pl.pallas_call(
        _in_proj_kernel,
        grid=(n // tm,),
        in_specs=[
            pl.BlockSpec((tm, D_MODEL), lambda i: (i, 0)),
            pl.BlockSpec((1, D_MODEL), const),
            pl.BlockSpec((nqv, D_MODEL), const),
            pl.BlockSpec((D_MODEL, nk), const),
            pl.BlockSpec((D_MODEL, LANES), const),
            pl.BlockSpec((1, LANES), const),
            pl.BlockSpec((FOX_WIDTH, LANES), const),
        ],
        out_specs=[
            pl.BlockSpec((nqv, tm), lambda i: (0, i)),
            pl.BlockSpec((tm, nk), lambda i: (i, 0)),
            pl.BlockSpec((tm, LANES), lambda i: (i, 0)),
            pl.BlockSpec((1, 2 * FOX_HEADS, LANES), lambda i: (i, 0, 0)),
        ],
        out_shape=[
            jax.ShapeDtypeStruct((nqv, n), jnp.bfloat16),
            jax.ShapeDtypeStruct((n, nk), jnp.bfloat16),
            jax.ShapeDtypeStruct((n, LANES), jnp.float32),
            jax.ShapeDtypeStruct((n // tm, 2 * FOX_HEADS, LANES), jnp.float32),
        ],
        compiler_params=pltpu.CompilerParams(
            dimension_semantics=("arbitrary",), vmem_limit_bytes=VMEM_LIMIT),
        name="in_proj",
    )(x2d, g, wqv_t, wk, wf, bf, head_of_col.astype(jnp.bfloat16))


def _decay_scan_kernel(x_ref, o_ref, *, rows_per_seq):
    x = x_ref[...]
    lane = lax.broadcasted_iota(jnp.int32, x.shape, 1)
    d = 1
    while d < LANES:
        x = x + jnp.where(lane >= d, pltpu.roll(x, d, 1), 0.0)
        d *= 2
    tot = jnp.broadcast_to(x[:, LANES - 1:LANES], x.shape)
    row = lax.broadcasted_iota(jnp.int32, x.shape, 0) % rows_per_seq
    inc = tot
    d = 1
    while d < rows_per_seq:
        inc = inc + jnp.where(row >= d, pltpu.roll(inc, d, 0), 0.0)
        d *= 2
    o_ref[...] = x + (inc - tot)


def _decay_scan(lf_rows, rows_per_seq):
    return pl.pallas_call(
        functools.partial(_decay_scan_kernel, rows_per_seq=rows_per_seq),
        out_shape=jax.ShapeDtypeStruct(lf_rows.shape, jnp.float32),
        compiler_params=pltpu.CompilerParams(vmem_limit_bytes=VMEM_LIMIT),
        name="decay_scan",
    )(lf_rows)


def _ones_rows(t):
    return jnp.where(lax.broadcasted_iota(jnp.int32, (SUM_ROWS, t), 0) == 0, 1.0, 0.0
                     ).astype(jnp.bfloat16)


def _flash_units(n_groups, t, kinds, score_fn, value_fn, s_bufs, m_ref, acc_ref):
    units = [(g, slice(c * t, (c + 1) * t), kind) for g in range(n_groups)
             for c, kind in enumerate(kinds) if kind is not None]

    depth = len(s_bufs)

    def produce(n):
        g, cols, kind = units[n]
        s = score_fn(g, cols, kind)
        s_bufs[n % depth][...] = s
        return jnp.max(s, axis=0, keepdims=True)

    mx = [produce(n) for n in range(min(depth - 1, len(units)))]
    for n, (g, cols, _) in enumerate(units):
        if n + depth - 1 < len(units):
            mx.append(produce(n + depth - 1))
        m_prev = m_ref[g, :, cols]
        m_new = jnp.maximum(m_prev, mx[n])
        m_ref[g, :, cols] = m_new
        p = jnp.exp2(s_bufs[n % depth][...] - m_new).astype(jnp.bfloat16)
        pv = jnp.dot(value_fn(g), p, preferred_element_type=jnp.float32)
        acc_ref[g, :, cols] = jnp.exp2(m_prev - m_new) * acc_ref[g, :, cols] + pv


def _init_stats(m_ref, acc_ref):
    m_ref[...] = jnp.full(m_ref.shape, NEG_BIG, jnp.float32)
    acc_ref[...] = jnp.zeros(acc_ref.shape, jnp.float32)


def _fox_kernel(i_ref, j_ref, j0_ref, qt_ref, vt_ref, k_ref, dec_ref, o_ref,
                *scratch):
    *s_bufs, m_ref, acc_ref = scratch
    s0_ref = s_bufs[0]
    p_id = pl.program_id(1)
    t = s0_ref.shape[0]
    parts = qt_ref.shape[1] // t
    rel = j_ref[p_id] - parts * i_ref[p_id]
    j = j_ref[p_id]
    step = pl.program_id(0) * pl.num_programs(1) + p_id
    n_steps = pl.num_programs(0) * pl.num_programs(1)
    first = [j0_ref[c * n_steps + step] for c in range(parts)]

    @pl.when(rel == parts - 1)
    def _():
        _init_stats(m_ref, acc_ref)

    ones = _ones_rows(t)
    zeros = jnp.zeros((HEAD_DIM, t), jnp.bfloat16)
    dec_row = lax.broadcasted_iota(jnp.int32, (LANES, t), 0)

    def rows(h):
        return slice(h * HEAD_DIM, (h + 1) * HEAD_DIM)

    def scores(h, cols, kind):
        q = qt_ref[rows(h), cols]
        pick = ((dec_row >= N_DECAY_PARTS * h) & (dec_row < N_DECAY_PARTS * (h + 1)))
        qa = jnp.concatenate(([q, zeros] if h % 2 == 0 else [zeros, q])
                             + [jnp.where(pick, 1.0, 0.0).astype(jnp.bfloat16)], axis=0)
        pair = slice((h // 2) * LANES, (h // 2 + 1) * LANES)
        ka = jnp.concatenate([k_ref[:, pair], dec_ref[0]], axis=1)
        s = jnp.dot(ka, qa, preferred_element_type=jnp.float32)
        if kind == "diag":
            kpos = lax.broadcasted_iota(jnp.int32, s.shape, 0)
            qpos = lax.broadcasted_iota(jnp.int32, s.shape, 1)
            s = jnp.where(kpos <= qpos, s, NEG_BIG)
        return s

    def values(h):
        return jnp.concatenate([vt_ref[rows(h), :], ones], axis=0)

    def sweep(kinds):
        _flash_units(FOX_HEADS, t, kinds, scores, values, s_bufs, m_ref, acc_ref)

    for r in range(parts):
        @pl.when(rel == r)
        def _(r=r):
            sweep([None] * r + ["diag"] + ["past"] * (parts - 1 - r))

    for a in range(1, parts + 1):
        cond = (rel < 0) & (j >= first[a - 1])
        if a < parts:
            cond = cond & (j < first[a])

        @pl.when(cond)
        def _(a=a):
            sweep(["past"] * a + [None] * (parts - a))

    @pl.when(j == 0)
    def _():
        for h in range(FOX_HEADS):
            a = acc_ref[h]
            o_ref[rows(h), :] = (a[:HEAD_DIM] / a[HEAD_DIM:HEAD_DIM + 1]).astype(o_ref.dtype)


def _fox_attn(zt, zk, dec, pairs_i, pairs_j, first_tile, batch, seq, t, parts):
    nt = seq // t
    tq = parts * t
    nq = seq // tq
    n = batch * seq
    n_pairs = pairs_i.shape[0]

    def key_tile(b, p, jj, j0):
        return jnp.maximum(jj[p], j0[b * n_pairs + p])

    grid_spec = pltpu.PrefetchScalarGridSpec(
        num_scalar_prefetch=3,
        grid=(batch, n_pairs),
        in_specs=[
            pl.BlockSpec((FOX_WIDTH, tq), lambda b, p, ii, jj, j0: (0, b * nq + ii[p])),
            pl.BlockSpec((FOX_WIDTH, t),
                         lambda b, p, ii, jj, j0: (1, b * nt + key_tile(b, p, jj, j0))),
            pl.BlockSpec((t, FOX_WIDTH),
                         lambda b, p, ii, jj, j0: (b * nt + key_tile(b, p, jj, j0), 0)),
            pl.BlockSpec((1, t, LANES), lambda b, p, ii, jj, j0: (b, key_tile(b, p, jj, j0), 0)),
        ],
        out_specs=pl.BlockSpec((FOX_WIDTH, tq), lambda b, p, ii, jj, j0: (0, b * nq + ii[p])),
        scratch_shapes=[pltpu.VMEM((t, t), jnp.float32)] * SCORE_BUFS + [
            pltpu.VMEM((FOX_HEADS, 1, tq), jnp.float32),
            pltpu.VMEM((FOX_HEADS, HEAD_DIM + SUM_ROWS, tq), jnp.float32),
        ],
    )
    return pl.pallas_call(
        _fox_kernel,
        grid_spec=grid_spec,
        out_shape=jax.ShapeDtypeStruct((FOX_WIDTH, n), jnp.bfloat16),
        compiler_params=pltpu.CompilerParams(
            dimension_semantics=("arbitrary", "arbitrary"), vmem_limit_bytes=VMEM_LIMIT),
        name="fox_attn",
    )(pairs_i, pairs_j, first_tile, zt, zt, zk, dec)


def _fox_first_tiles(stats, pieces, pairs_i, batch, seq, t, parts):
    nt = seq // t
    nq = nt // parts
    f32 = jnp.float32
    aq = jnp.sqrt(stats[:, :FOX_HEADS, 0].reshape(batch, nt, FOX_HEADS)) * NORM_SLACK
    ak = jnp.sqrt(stats[:, FOX_HEADS, :FOX_HEADS].reshape(batch, nt, FOX_HEADS)) * NORM_SLACK
    d = sum(p.astype(f32) for p in pieces).reshape(batch, FOX_HEADS, nt, t)
    d_first = d[..., 0].transpose(0, 2, 1)
    d_last = d[..., t - 1].transpose(0, 2, 1)
    row_floor = d_first - aq * ak
    tile_top = aq[:, :, None, :] * ak[:, None, :, :] + d_last[:, None, :, :]
    alive = jnp.any(tile_top >= row_floor[:, :, None, :] - DEAD_GAP, axis=-1)
    tile = jnp.arange(nt)
    alive = alive | (tile[None, None, :] >= tile[None, :, None])
    first = jnp.argmax(alive, axis=-1).astype(jnp.int32).reshape(batch, nq, parts)
    first = lax.cummin(first, axis=2, reverse=True)
    return first[:, pairs_i].transpose(2, 0, 1).reshape(-1)


def _diff_kernel(i_ref, j_ref, qt_ref, vt_ref, kd_ref, bd_ref, bs_ref, lam_ref, g_ref,
                 o_ref, *scratch, lambda_init):
    *s_bufs, m_ref, acc_ref = scratch
    s0_ref = s_bufs[0]
    p_id = pl.program_id(1)
    t = s0_ref.shape[0]
    parts = qt_ref.shape[1] // t
    j = j_ref[p_id]
    rel = j - parts * i_ref[p_id]

    @pl.when(j == 0)
    def _():
        _init_stats(m_ref, acc_ref)

    first_map = lax.broadcasted_iota(jnp.int32, (DIFF_V_DIM, t), 0) < HEAD_DIM
    ones = _ones_rows(t)

    def rows(h):
        return slice(h * DIFF_V_DIM, (h + 1) * DIFF_V_DIM)

    def scores(u, cols, kind):
        h, mp = divmod(u, 2)
        q = qt_ref[rows(h), cols]
        zero = jnp.zeros_like(q)
        qm = jnp.where(first_map, q, zero) if mp == 0 else jnp.where(first_map, zero, q)
        kd = kd_ref[:, rows(h)]
        if kind == "sub":
            s_far = jnp.dot(kd[:t - LANES], qm, preferred_element_type=jnp.float32)
            s_near = jnp.dot(kd[t - LANES:], qm, preferred_element_type=jnp.float32)
            return jnp.concatenate([s_far, s_near + bs_ref[h]], axis=0)
        s = jnp.dot(kd, qm, preferred_element_type=jnp.float32)
        if kind == "diag":
            s = s + bd_ref[h]
        return s

    def values(u):
        return jnp.concatenate([vt_ref[rows(u // 2), :], ones], axis=0)

    def kind_of(d):
        return None if d > 0 else "diag" if d == 0 else "sub" if d == -1 else "far"

    def sweep(r):
        kinds = [kind_of(r - c) for c in range(parts)]
        _flash_units(2 * DIFF_HEADS, t, kinds, scores, values, s_bufs, m_ref, acc_ref)

    @pl.when(rel < -1)
    def _():
        sweep(-2)

    for r in range(-1, parts):
        @pl.when(rel == r)
        def _(r=r):
            sweep(r)

    @pl.when(rel == parts - 1)
    def _():
        lam_rows = lam_ref[...]
        dot1 = jnp.sum(lam_rows[0:1] * lam_rows[1:2], axis=-1, keepdims=True)
        dot2 = jnp.sum(lam_rows[2:3] * lam_rows[3:4], axis=-1, keepdims=True)
        lam = jnp.exp(dot1) - jnp.exp(dot2) + lambda_init
        g = g_ref[...] * (1.0 - lambda_init)
        for h in range(DIFF_HEADS):
            a1 = acc_ref[2 * h]
            a2 = acc_ref[2 * h + 1]
            d = (a1[:DIFF_V_DIM] / a1[DIFF_V_DIM:DIFF_V_DIM + 1]
                 - lam * (a2[:DIFF_V_DIM] / a2[DIFF_V_DIM:DIFF_V_DIM + 1]))
            y = d * lax.rsqrt(jnp.mean(d * d, axis=0, keepdims=True) + SUBLN_EPS)
            o_ref[rows(h), :] = (y * g).astype(o_ref.dtype)


def _diff_attn(zt, kd, bias_diag, bias_sub, lam_rows, g_col, pairs_i, pairs_j,
               batch, seq, t, parts, lambda_init):
    nt = seq // t
    tq = parts * t
    nq = seq // tq
    n = batch * seq
    grid_spec = pltpu.PrefetchScalarGridSpec(
        num_scalar_prefetch=2,
        grid=(batch, pairs_i.shape[0]),
        in_specs=[
            pl.BlockSpec((DIFF_WIDTH, tq), lambda b, p, ii, jj: (2, b * nq + ii[p])),
            pl.BlockSpec((DIFF_WIDTH, t), lambda b, p, ii, jj: (3, b * nt + jj[p])),
            pl.BlockSpec((t, DIFF_WIDTH), lambda b, p, ii, jj: (b * nt + jj[p], 1)),
            pl.BlockSpec((DIFF_HEADS, t, t), lambda b, p, ii, jj: (0, 0, 0)),
            pl.BlockSpec((DIFF_HEADS, LANES, t), lambda b, p, ii, jj: (0, 0, 0)),
            pl.BlockSpec((8, LANES), lambda b, p, ii, jj: (0, 0)),
            pl.BlockSpec((DIFF_V_DIM, 1), lambda b, p, ii, jj: (0, 0)),
        ],
        out_specs=pl.BlockSpec((DIFF_WIDTH, tq), lambda b, p, ii, jj: (0, b * nq + ii[p])),
        scratch_shapes=[pltpu.VMEM((t, t), jnp.float32)] * SCORE_BUFS + [
            pltpu.VMEM((2 * DIFF_HEADS, 1, tq), jnp.float32),
            pltpu.VMEM((2 * DIFF_HEADS, DIFF_V_DIM + SUM_ROWS, tq), jnp.float32),
        ],
    )
    return pl.pallas_call(
        functools.partial(_diff_kernel, lambda_init=lambda_init),
        grid_spec=grid_spec,
        out_shape=jax.ShapeDtypeStruct((DIFF_WIDTH, n), jnp.bfloat16),
        compiler_params=pltpu.CompilerParams(
            dimension_semantics=("arbitrary", "arbitrary"), vmem_limit_bytes=VMEM_LIMIT),
        name="diff_attn",
    )(pairs_i, pairs_j, zt, zt, kd, bias_diag, bias_sub, lam_rows, g_col)


def _out_mlp_kernel(x_ref, fox_ref, dif_ref, wo_ref, g_ref, w1_ref, w2_ref, gf_ref, o_ref,
                    *, final_norm):
    y = lax.dot_general(fox_ref[...], wo_ref[:FOX_WIDTH, :], _TN,
                        preferred_element_type=jnp.float32)
    y = y + lax.dot_general(dif_ref[...], wo_ref[FOX_WIDTH:, :], _TN,
                            preferred_element_type=jnp.float32)
    x1 = x_ref[...] + y
    h = _rms(x1, g_ref[...], NORM_EPS).astype(jnp.bfloat16)
    acc = x1
    for c in range(D_FF // FF_CHUNK):
        u = jnp.dot(h, w1_ref[:, c * FF_CHUNK:(c + 1) * FF_CHUNK],
                    preferred_element_type=jnp.float32)
        u = jnp.square(jnp.maximum(u, 0.0)).astype(jnp.bfloat16)
        acc = acc + jnp.dot(u, w2_ref[c * FF_CHUNK:(c + 1) * FF_CHUNK, :],
                            preferred_element_type=jnp.float32)
    if final_norm:
        acc = _rms(acc, gf_ref[...], NORM_EPS)
    o_ref[...] = acc


def _out_mlp(x2d, fox_t, dif_t, wo, g, w1, w2, gf, tm, final_norm):
    n = x2d.shape[0]
    const = lambda i: (0, 0)
    return pl.pallas_call(
        functools.partial(_out_mlp_kernel, final_norm=final_norm),
        grid=(n // tm,),
        in_specs=[
            pl.BlockSpec((tm, D_MODEL), lambda i: (i, 0)),
            pl.BlockSpec((FOX_WIDTH, tm), lambda i: (0, i)),
            pl.BlockSpec((DIFF_WIDTH, tm), lambda i: (0, i)),
            pl.BlockSpec((D_MODEL, D_MODEL), const, pipeline_mode=pl.Buffered(1)),
            pl.BlockSpec((1, D_MODEL), const),
            pl.BlockSpec((D_MODEL, D_FF), const, pipeline_mode=pl.Buffered(1)),
            pl.BlockSpec((D_FF, D_MODEL), const, pipeline_mode=pl.Buffered(1)),
            pl.BlockSpec((1, D_MODEL), const),
        ],
        out_specs=pl.BlockSpec((tm, D_MODEL), lambda i: (i, 0)),
        out_shape=jax.ShapeDtypeStruct((n, D_MODEL), jnp.float32),
        compiler_params=pltpu.CompilerParams(
            dimension_semantics=("arbitrary",), vmem_limit_bytes=VMEM_LIMIT),
        name="out_mlp",
    )(x2d, fox_t, dif_t, wo, g, w1, w2, gf)


def _t5_bucket(rel):
    half = N_BUCKETS // 2
    max_exact = half // 2
    ret = jnp.where(rel > 0, half, 0)
    n = jnp.abs(rel)
    nf = jnp.maximum(n, 1).astype(jnp.float32)
    large = max_exact + (jnp.log(nf / max_exact) / math.log(MAX_DISTANCE / max_exact)
                         * (half - max_exact)).astype(jnp.int32)
    large = jnp.minimum(large, half - 1)
    return ret + jnp.where(n < max_exact, n, large)


def _bias_tiles(table, t):
    tab = table.astype(jnp.float32)

    def lookup(rel):
        bucket = _t5_bucket(rel)
        out = jnp.zeros((tab.shape[1],) + rel.shape, jnp.float32)
        for b in range(N_BUCKETS):
            out = jnp.where(bucket == b, tab[b].reshape((-1,) + (1,) * rel.ndim), out)
        return out

    far = lookup(jnp.full((1, 1), -MAX_DISTANCE, jnp.int32))
    k = jnp.arange(t)[:, None]
    q = jnp.arange(t)[None, :]
    rel = k - q
    diag = (lookup(rel) - far) * LOG2E
    diag = jnp.where((k // CHUNK) <= (q // CHUNK), diag, NEG_BIG)
    rel_sub = rel[t - LANES:] - t
    sub = (lookup(rel_sub) - far) * LOG2E
    return diag, sub


def _split_bf16(x, parts):
    out = []
    for _ in range(parts - 1):
        bits = lax.bitcast_convert_type(x, jnp.uint32) & jnp.uint32(0xFFFF0000)
        piece = lax.bitcast_convert_type(bits, jnp.float32)
        out.append(piece.astype(jnp.bfloat16))
        x = x - piece
    out.append(x.astype(jnp.bfloat16))
    return out


def _forward(x, w_in, b_f, lambda_q1, lambda_k1, lambda_q2, lambda_k2, subln_g, w_out,
             norm_attn_g, norm_mlp_g, w_mlp_in, w_mlp_out, rel_bias_table, final_norm_g,
             *, attn_tile, row_tile):
    batch, seq, _ = x.shape
    depth = w_in.shape[0]
    n = batch * seq
    t = attn_tile
    tm = min(row_tile, n)
    qp, dp = FOX_Q_PARTS, DIFF_Q_PARTS
    assert seq % (qp * t) == 0 and seq % (dp * t) == 0 and n % tm == 0
    assert t % LANES == 0 and t > LANES
    pairs = [(i, j) for i in range(seq // (dp * t)) for j in range(dp * (i + 1))]
    pairs_i = jnp.asarray([p[0] for p in pairs], jnp.int32)
    pairs_j = jnp.asarray([p[1] for p in pairs], jnp.int32)
    fox_pairs = [(i, j) for i in range(seq // (qp * t))
                 for j in range(qp * (i + 1) - 1, -1, -1)]
    fox_i = jnp.asarray([p[0] for p in fox_pairs], jnp.int32)
    fox_j = jnp.asarray([p[1] for p in fox_pairs], jnp.int32)
    bias_diag, bias_sub = _bias_tiles(rel_bias_table, t)

    o_fq, o_fk, o_fv = 0, FOX_WIDTH, 2 * FOX_WIDTH
    o_ff = 3 * FOX_WIDTH
    o_dq = o_ff + FOX_HEADS
    o_dk = o_dq + DIFF_WIDTH
    o_dv = o_dk + DIFF_WIDTH
    qscale = LOG2E / math.sqrt(HEAD_DIM)

    x2d = x.reshape(n, D_MODEL)
    for l in range(depth):
        w = w_in[l]
        wqv_t = jnp.concatenate([
            w[:, o_fq:o_fq + FOX_WIDTH] * qscale, w[:, o_fv:o_fv + FOX_WIDTH],
            w[:, o_dq:o_dq + DIFF_WIDTH] * qscale, w[:, o_dv:o_dv + DIFF_WIDTH]],
            axis=1).T.astype(jnp.bfloat16)
        wk = jnp.concatenate([w[:, o_fk:o_fk + FOX_WIDTH], w[:, o_dk:o_dk + DIFF_WIDTH]],
                             axis=1).astype(jnp.bfloat16)
        wf = jnp.pad(w[:, o_ff:o_ff + FOX_HEADS], ((0, 0), (0, LANES - FOX_HEADS))
                     ).astype(jnp.bfloat16)
        bf = jnp.pad(b_f[l].astype(jnp.float32), (0, LANES - FOX_HEADS)).reshape(1, LANES)

        zt, zk, lf, stats = _in_proj(x2d, norm_attn_g[l].reshape(1, D_MODEL), wqv_t, wk, wf, bf, t)

        lf_rows = lf[:, :FOX_HEADS].reshape(batch, seq, FOX_HEADS).transpose(0, 2, 1)
        c = _decay_scan(lf_rows.reshape(batch * FOX_HEADS * seq // LANES, LANES), seq // LANES)
        c = c.reshape(batch, FOX_HEADS, seq)
        parts = _split_bf16(c * (-LOG2E), N_DECAY_PARTS)
        dec = jnp.stack(parts, axis=2).transpose(0, 3, 1, 2)
        dec = jnp.pad(dec.reshape(batch, seq, FOX_HEADS * N_DECAY_PARTS),
                      ((0, 0), (0, 0), (0, LANES - FOX_HEADS * N_DECAY_PARTS)))

        first_tile = _fox_first_tiles(stats, parts, fox_i, batch, seq, t, qp)
        fox_t = _fox_attn(zt, zk, dec, fox_i, fox_j, first_tile, batch, seq, t, qp)

        lambda_init = 0.8 - 0.6 * math.exp(-0.3 * l)
        lam_rows = jnp.pad(
            jnp.stack([lambda_q1[l], lambda_k1[l], lambda_q2[l], lambda_k2[l]]).astype(jnp.float32),
            ((0, 4), (0, LANES - HEAD_DIM)))
        dif_t = _diff_attn(zt, zk, bias_diag, bias_sub, lam_rows,
                           subln_g[l].astype(jnp.float32).reshape(DIFF_V_DIM, 1),
                           pairs_i, pairs_j, batch, seq, t, dp, lambda_init)

        x2d = _out_mlp(x2d, fox_t, dif_t, w_out[l].astype(jnp.bfloat16),
                       norm_mlp_g[l].reshape(1, D_MODEL), w_mlp_in[l].astype(jnp.bfloat16),
                       w_mlp_out[l].astype(jnp.bfloat16), final_norm_g.reshape(1, D_MODEL),
                       tm, final_norm=(l == depth - 1))
    return x2d.reshape(batch, seq, D_MODEL)


def kernel(x, w_in, b_f, lambda_q1, lambda_k1, lambda_q2, lambda_k2, subln_g, w_out,
           norm_attn_g, norm_mlp_g, w_mlp_in, w_mlp_out, rel_bias_table, final_norm_g):
    return _forward(x, w_in, b_f, lambda_q1, lambda_k1, lambda_q2, lambda_k2, subln_g, w_out,
                    norm_attn_g, norm_mlp_g, w_mlp_in, w_mlp_out, rel_bias_table, final_norm_g,
                    attn_tile=ATTN_TILE, row_tile=ROW_TILE)
```
